```python
import math
import jax, jax.numpy as jnp
from jax import lax
import numpy as np

D_MODEL = 2048
BATCH = 4
SEQ = 4096
DEPTH = 4

N_EVEN = (DEPTH + 1) // 2
N_ODD = DEPTH // 2

HEAD_DIM = 64
N_Q_HEADS = D_MODEL // HEAD_DIM
N_KV_HEADS = max(1, N_Q_HEADS // 8)
GQA_GROUP = N_Q_HEADS // N_KV_HEADS
QKV_WIDTH = (N_Q_HEADS + 2 * N_KV_HEADS) * HEAD_DIM
WINDOW = 128
ATTN_BLOCK = 128
ROT_DIM = HEAD_DIM // 4
ROPE_THETA = 500000.0

S5_GROUP = 16
S5_GROUPS = D_MODEL // S5_GROUP
S5_STATE = 64
S5_CHUNK = 128

D_FF = D_MODEL * 7 // 2
N_EXPERTS = 8
TOP_K = 2

PLE_DIM = 256
RMS_EPS = 1e-6

kernel_name = "hybrid_swa_s5_moe_trunk"


def rms_norm(x, g):
    xf = x.astype(jnp.float32)
    y = xf * lax.rsqrt(jnp.mean(xf * xf, axis=-1, keepdims=True) + RMS_EPS)
    return (y * g.astype(jnp.float32)).astype(x.dtype)


def rope_tables(positions):
    inv_freq = jnp.power(ROPE_THETA, -jnp.arange(0, ROT_DIM, 2, dtype=jnp.float32) / ROT_DIM)
    ang = positions.astype(jnp.float32)[..., None] * inv_freq
    return jnp.cos(ang)[:, :, None, :], jnp.sin(ang)[:, :, None, :]


def apply_partial_rope(x, cos, sin):
    half = ROT_DIM // 2
    cos = cos.astype(x.dtype)
    sin = sin.astype(x.dtype)
    x1 = x[..., :half]
    x2 = x[..., half:ROT_DIM]
    return jnp.concatenate([x1 * cos - x2 * sin, x2 * cos + x1 * sin, x[..., ROT_DIM:]], axis=-1)


def sliding_window_attention(u, w_qkv, b_qkv, w_o, b_o, sinks, cos, sin):
    bsz, seq, _ = u.shape
    nb = seq // ATTN_BLOCK
    qkv = u @ w_qkv + b_qkv
    q_end = N_Q_HEADS * HEAD_DIM
    k_end = q_end + N_KV_HEADS * HEAD_DIM
    q = qkv[..., :q_end].reshape(bsz, seq, N_Q_HEADS, HEAD_DIM)
    k = qkv[..., q_end:k_end].reshape(bsz, seq, N_KV_HEADS, HEAD_DIM)
    v = qkv[..., k_end:].reshape(bsz, seq, N_KV_HEADS, HEAD_DIM)
    q = apply_partial_rope(q, cos, sin)
    k = apply_partial_rope(k, cos, sin)
    q = q.reshape(bsz, nb, ATTN_BLOCK, N_KV_HEADS, GQA_GROUP, HEAD_DIM)

    def band(t):
        cur = t.reshape(bsz, nb, ATTN_BLOCK, N_KV_HEADS, HEAD_DIM)
        prev = jnp.pad(t, ((0, 0), (ATTN_BLOCK, 0), (0, 0), (0, 0)))[:, :seq]
        prev = prev.reshape(bsz, nb, ATTN_BLOCK, N_KV_HEADS, HEAD_DIM)
        return jnp.concatenate([prev, cur], axis=2)

    kb, vb = band(k), band(v)
    scores = jnp.einsum("bnqhgd,bnkhd->bnhgqk", q, kb).astype(jnp.float32) * (HEAD_DIM ** -0.5)
    qi = jnp.arange(ATTN_BLOCK)[:, None]
    ki = jnp.arange(2 * ATTN_BLOCK)[None, :]
    rel = qi + ATTN_BLOCK - ki
    in_window = (rel >= 0) & (rel < WINDOW)
    not_pad = (jnp.arange(nb)[:, None, None] > 0) | (ki[None] >= ATTN_BLOCK)
    mask = in_window[None] & not_pad
    scores = jnp.where(mask[None, :, None, None], scores, -jnp.inf)
    sink = sinks.astype(jnp.float32).reshape(N_KV_HEADS, GQA_GROUP)[None, None, :, :, None, None]
    m = jnp.maximum(jnp.max(scores, axis=-1, keepdims=True), sink)
    e = jnp.exp(scores - m)
    probs = e / (jnp.sum(e, axis=-1, keepdims=True) + jnp.exp(sink - m))
    o = jnp.einsum("bnhgqk,bnkhd->bnqhgd", probs.astype(vb.dtype), vb)
    o = o.reshape(bsz, seq, N_Q_HEADS * HEAD_DIM)
    return o @ w_o + b_o


def _complex_affine_combine(e1, e2):
    a1r, a1i, x1r, x1i = e1
    a2r, a2i, x2r, x2i = e2
    return (a2r * a1r - a2i * a1i,
            a2r * a1i + a2i * a1r,
            a2r * x1r - a2i * x1i + x2r,
            a2r * x1i + a2i * x1r + x2i)


def s5_layer(u, a_re, a_im, log_dt, b_re, b_im, c_re, c_im, d_skip, w_glu, b_glu):
    bsz, seq, _ = u.shape
    f32 = jnp.float32
    nc = seq // S5_CHUNK
    dt = jnp.exp(log_dt.astype(f32))[:, None]
    ar = a_re.astype(f32)
    ai = a_im.astype(f32)
    mag = jnp.exp(ar * dt)
    lr = mag * jnp.cos(ai * dt)
    li = mag * jnp.sin(ai * dt)
    den = ar * ar + ai * ai
    zr = ((lr - 1.0) * ar + li * ai) / den
    zi = (li * ar - (lr - 1.0) * ai) / den
    br = b_re.astype(f32)
    bi = b_im.astype(f32)
    bbr = zr[..., None] * br - zi[..., None] * bi
    bbi = zr[..., None] * bi + zi[..., None] * br
    cr = c_re.astype(f32)
    ci = c_im.astype(f32)

    uf = u.astype(f32)
    ug = uf.reshape(bsz, nc, S5_CHUNK, S5_GROUPS, S5_GROUP).transpose(1, 0, 2, 3, 4)

    def segment(carry, uc):
        hr, hi = carry
        xr = jnp.einsum("bcgh,gnh->bcgn", uc, bbr)
        xi = jnp.einsum("bcgh,gnh->bcgn", uc, bbi)
        lam_r = jnp.broadcast_to(lr, xr.shape)
        lam_i = jnp.broadcast_to(li, xr.shape)
        pr, pi_, sr0, si0 = lax.associative_scan(_complex_affine_combine, (lam_r, lam_i, xr, xi), axis=1)
        sr = pr * hr[:, None] - pi_ * hi[:, None] + sr0
        si = pr * hi[:, None] + pi_ * hr[:, None] + si0
        y = jnp.einsum("bcgn,ghn->bcgh", sr, cr) - jnp.einsum("bcgn,ghn->bcgh", si, ci)
        return (sr[:, -1], si[:, -1]), y

    h0 = (jnp.zeros((bsz, S5_GROUPS, S5_STATE), f32), jnp.zeros((bsz, S5_GROUPS, S5_STATE), f32))
    _, y = lax.scan(segment, h0, ug)
    y = y.transpose(1, 0, 2, 3, 4).reshape(bsz, seq, D_MODEL) + d_skip.astype(f32) * uf
    y = jax.nn.gelu(y, approximate=False).astype(u.dtype)
    z = y @ w_glu + b_glu
    za, zb = jnp.split(z, 2, axis=-1)
    return za * jax.nn.sigmoid(zb)


def swiglu(u, w_gate_up, w_down):
    g, up = jnp.split(u @ w_gate_up, 2, axis=-1)
    return (jax.nn.silu(g) * up) @ w_down


def moe_swiglu(u, w_router, b_router, w_gate_up, w_down):
    logits = (u @ w_router).astype(jnp.float32) + b_router.astype(jnp.float32)
    vals, idx = lax.top_k(logits, TOP_K)
    wts = jax.nn.softmax(vals, axis=-1)
    gates = jnp.sum(jax.nn.one_hot(idx, N_EXPERTS, dtype=jnp.float32) * wts[..., None], axis=-2)
    gates = gates.astype(u.dtype)
    out = jnp.zeros_like(u)
    for e in range(N_EXPERTS):
        out = out + gates[..., e:e + 1] * swiglu(u, w_gate_up[e], w_down[e])
    return out


def setup_inputs(seed: int = 0) -> dict:
    key = jax.random.key(seed)
    ks = iter(jax.random.split(key, 40))
    f32 = jnp.float32

    def nrm(shape, scale):
        return jax.random.normal(next(ks), shape, f32) * scale

    def gain(shape):
        return 1.0 + nrm(shape, 0.01)

    x = nrm((BATCH, SEQ, D_MODEL), 1.0)
    p = nrm((DEPTH, BATCH, SEQ, PLE_DIM), 1.0)
    offsets = jax.random.randint(next(ks), (BATCH, 1), 0, 1024, dtype=jnp.int32)
    positions = offsets + jnp.arange(SEQ, dtype=jnp.int32)[None, :]

    n_idx = jnp.arange(S5_STATE, dtype=f32)
    log_dt = jax.random.uniform(next(ks), (N_ODD, S5_GROUPS), f32, math.log(1e-3), math.log(1e-1))

    return {
        "x": x,
        "p": p,
        "positions": positions,
        "norm_mix": gain((DEPTH, D_MODEL)),
        "norm_ffn": gain((DEPTH, D_MODEL)),
        "norm_ple": gain((DEPTH, D_MODEL)),
        "norm_final": gain((D_MODEL,)),
        "attn_w_qkv": nrm((N_EVEN, D_MODEL, QKV_WIDTH), D_MODEL ** -0.5),
        "attn_b_qkv": nrm((N_EVEN, QKV_WIDTH), 0.01),
        "attn_w_o": nrm((N_EVEN, N_Q_HEADS * HEAD_DIM, D_MODEL), (N_Q_HEADS * HEAD_DIM) ** -0.5),
        "attn_b_o": nrm((N_EVEN, D_MODEL), 0.01),
        "attn_sinks": nrm((N_EVEN, N_Q_HEADS), 1.0),
        "s5_a_re": -0.5 + nrm((N_ODD, S5_GROUPS, S5_STATE), 0.01),
        "s5_a_im": math.pi * n_idx[None, None, :] + nrm((N_ODD, S5_GROUPS, S5_STATE), 0.01),
        "s5_log_dt": log_dt,
        "s5_b_re": nrm((N_ODD, S5_GROUPS, S5_STATE, S5_GROUP), (2 * S5_GROUP) ** -0.5),
        "s5_b_im": nrm((N_ODD, S5_GROUPS, S5_STATE, S5_GROUP), (2 * S5_GROUP) ** -0.5),
        "s5_c_re": nrm((N_ODD, S5_GROUPS, S5_GROUP, S5_STATE), S5_STATE ** -0.5),
        "s5_c_im": nrm((N_ODD, S5_GROUPS, S5_GROUP, S5_STATE), S5_STATE ** -0.5),
        "s5_d": nrm((N_ODD, D_MODEL), 1.0),
        "s5_w_glu": nrm((N_ODD, D_MODEL, 2 * D_MODEL), D_MODEL ** -0.5),
        "s5_b_glu": nrm((N_ODD, 2 * D_MODEL), 0.01),
        "ffn_w_gate_up": nrm((N_EVEN, D_MODEL, 2 * D_FF), D_MODEL ** -0.5),
        "ffn_w_down": nrm((N_EVEN, D_FF, D_MODEL), D_FF ** -0.5),
        "moe_w_router": nrm((N_ODD, D_MODEL, N_EXPERTS), D_MODEL ** -0.5),
        "moe_b_router": nrm((N_ODD, N_EXPERTS), 0.01),
        "moe_w_gate_up": nrm((N_ODD, N_EXPERTS, D_MODEL, 2 * D_FF), D_MODEL ** -0.5),
        "moe_w_down": nrm((N_ODD, N_EXPERTS, D_FF, D_MODEL), D_FF ** -0.5),
        "ple_w_proj": nrm((DEPTH, PLE_DIM, D_MODEL), PLE_DIM ** -0.5),
        "ple_w_gate": nrm((DEPTH, D_MODEL, D_MODEL), D_MODEL ** -0.5),
    }


def reference(x, p, positions, norm_mix, norm_ffn, norm_ple, norm_final,
              attn_w_qkv, attn_b_qkv, attn_w_o, attn_b_o, attn_sinks,
              s5_a_re, s5_a_im, s5_log_dt, s5_b_re, s5_b_im, s5_c_re, s5_c_im,
              s5_d, s5_w_glu, s5_b_glu,
              ffn_w_gate_up, ffn_w_down,
              moe_w_router, moe_b_router, moe_w_gate_up, moe_w_down,
              ple_w_proj, ple_w_gate):
    cos, sin = rope_tables(positions)
    h = x
    for i in range(DEPTH):
        k = i // 2
        u = rms_norm(h, norm_mix[i])
        if i % 2 == 0:
            h = h + sliding_window_attention(u, attn_w_qkv[k], attn_b_qkv[k], attn_w_o[k],
                                             attn_b_o[k], attn_sinks[k], cos, sin)
        else:
            h = h + s5_layer(u, s5_a_re[k], s5_a_im[k], s5_log_dt[k], s5_b_re[k], s5_b_im[k],
                             s5_c_re[k], s5_c_im[k], s5_d[k], s5_w_glu[k], s5_b_glu[k])
        u = rms_norm(h, norm_ffn[i])
        if i % 2 == 0:
            h = h + swiglu(u, ffn_w_gate_up[k], ffn_w_down[k])
        else:
            h = h + moe_swiglu(u, moe_w_router[k], moe_b_router[k], moe_w_gate_up[k], moe_w_down[k])
        gate = jax.nn.sigmoid(rms_norm(h, norm_ple[i]) @ ple_w_gate[i])
        h = h + gate * (p[i].astype(h.dtype) @ ple_w_proj[i])
    return rms_norm(h, norm_final)
```

```python
import functools
import math

import jax
import jax.numpy as jnp
from jax import lax
from jax.experimental import pallas as pl
from jax.experimental.pallas import tpu as pltpu

F32 = jnp.float32
BF16 = jnp.bfloat16

HEAD_DIM = 64
KV_RATIO = 8
WINDOW = 128
ATTN_BLOCK = 128
ROT_DIM = HEAD_DIM // 4
ROPE_THETA = 500000.0
S5_GROUP = 16
S5_CHUNK = 8
N_EXPERTS = 8
RMS_EPS = 1e-6
LANES = 128
VMEM_LIMIT = 60 * 1024 * 1024


def _cparams(sem):
    return pltpu.CompilerParams(dimension_semantics=sem, vmem_limit_bytes=VMEM_LIMIT)


def _fit(tile, n):
    if n <= tile:
        return n
    return max(c for c in range(LANES, tile + 1, LANES) if n % c == 0)


def _mm_body(*refs, norm, n_rhs, n_bias, pair2, n_extra, nk, epilogue):
    it = iter(refs)
    x_ref = next(it)
    g_ref = next(it) if norm else None
    w_refs = [next(it) for _ in range(n_rhs)]
    b_refs = [next(it) for _ in range(n_bias)]
    x2_ref, w2_ref = (next(it), next(it)) if pair2 else (None, None)
    e_refs = [next(it) for _ in range(n_extra)]
    o_ref = next(it)
    xn_ref = next(it) if norm else None
    acc_refs = [next(it) for _ in range(n_rhs)] if nk > 1 else []

    j = pl.program_id(1)
    k = pl.program_id(2)

    if norm:
        @pl.when(j == 0)
        def _():
            x = x_ref[...].astype(F32)
            r = lax.rsqrt(jnp.mean(x * x, axis=-1, keepdims=True) + RMS_EPS)
            xn_ref[...] = ((x * r) * g_ref[...]).astype(BF16)
        lhs = xn_ref[...]
    else:
        lhs = x_ref[...].astype(BF16)

    parts = [jnp.dot(lhs, w[...], preferred_element_type=F32) for w in w_refs]

    def finish(accs):
        biases = [b[...] for b in b_refs]
        acc2 = None
        if pair2:
            acc2 = jnp.dot(x2_ref[...].astype(BF16), w2_ref[...], preferred_element_type=F32)
        o_ref[...] = epilogue(accs, biases, acc2, [e[...] for e in e_refs]).astype(o_ref.dtype)

    if nk == 1:
        finish(parts)
    else:
        @pl.when(k == 0)
        def _():
            for a, p_ in zip(acc_refs, parts):
                a[...] = p_

        @pl.when(k > 0)
        def _():
            for a, p_ in zip(acc_refs, parts):
                a[...] += p_

        @pl.when(k == nk - 1)
        def _():
            finish([a[...] for a in acc_refs])


def fused_mm(x, ws, *, tm, tn, n_out, epilogue, out_dtype, gain=None, biases=(),
             pair2=None, extras=(), tk=None, name=None):
    m, kdim = x.shape
    tm = _fit(tm, m)
    tn = _fit(tn, n_out)
    tk = kdim if tk is None else _fit(tk, kdim)
    norm = gain is not None
    assert m % tm == 0 and n_out % tn == 0 and kdim % tk == 0
    nk = kdim // tk
    assert not (norm and nk > 1)
    grid = (m // tm, n_out // tn, nk)

    in_specs = [pl.BlockSpec((tm, tk), lambda i, j, k: (i, k))]
    args = [x]
    if norm:
        in_specs.append(pl.BlockSpec((1, kdim), lambda i, j, k: (0, 0)))
        args.append(gain.reshape(1, kdim).astype(F32))
    for w, off in ws:
        assert off % tn == 0
        ob = off // tn
        in_specs.append(pl.BlockSpec((tk, tn), lambda i, j, k, ob=ob: (k, j + ob)))
        args.append(w)
    for b, off in biases:
        ob = off // tn
        in_specs.append(pl.BlockSpec((1, tn), lambda i, j, k, ob=ob: (0, j + ob)))
        args.append(b.reshape(1, -1).astype(F32))
    if pair2 is not None:
        x2, w2 = pair2
        k2 = x2.shape[1]
        in_specs.append(pl.BlockSpec((tm, k2), lambda i, j, k: (i, 0)))
        in_specs.append(pl.BlockSpec((k2, tn), lambda i, j, k: (0, j)))
        args += [x2, w2]
    for arr, kind in extras:
        if kind == "tile":
            in_specs.append(pl.BlockSpec((tm, tn), lambda i, j, k: (i, j)))
        elif kind == "rows":
            in_specs.append(pl.BlockSpec((tm, arr.shape[1]), lambda i, j, k: (i, 0)))
        else:
            in_specs.append(pl.BlockSpec((1, tn), lambda i, j, k: (0, j)))
        args.append(arr)

    scratch = []
    if norm:
        scratch.append(pltpu.VMEM((tm, kdim), BF16))
    if nk > 1:
        scratch += [pltpu.VMEM((tm, tn), F32) for _ in ws]

    body = functools.partial(_mm_body, norm=norm, n_rhs=len(ws), n_bias=len(biases),
                             pair2=pair2 is not None, n_extra=len(extras), nk=nk,
                             epilogue=epilogue)
    return pl.pallas_call(
        body,
        grid=grid,
        in_specs=in_specs,
        out_specs=pl.BlockSpec((tm, tn), lambda i, j, k: (i, j)),
        out_shape=jax.ShapeDtypeStruct((m, n_out), out_dtype),
        scratch_shapes=scratch,
        compiler_params=_cparams(("parallel", "arbitrary", "arbitrary")),
        name=name,
    )(*args)


def _rope_table_body(pos_ref, freq_ref, c_ref, s1_ref, s2_ref):
    ang = pos_ref[...].astype(F32) * freq_ref[...]
    lane = lax.broadcasted_iota(jnp.int32, ang.shape, 1) % HEAD_DIM
    cos = jnp.cos(ang)
    sin = jnp.sin(ang)
    half = ROT_DIM // 2
    c_ref[...] = jnp.where(lane < ROT_DIM, cos, 1.0)
    s1_ref[...] = jnp.where(lane < half, -sin, 0.0)
    s2_ref[...] = jnp.where((lane >= half) & (lane < ROT_DIM), sin, 0.0)


def rope_tables(positions):
    t = positions.size
    tm = min(1024, t)
    inv_freq = jnp.power(ROPE_THETA, -jnp.arange(0, ROT_DIM, 2, dtype=F32) / ROT_DIM)
    lane = jnp.arange(LANES) % HEAD_DIM
    freq = jnp.where(lane < ROT_DIM, inv_freq[lane % (ROT_DIM // 2)], 0.0).reshape(1, LANES)
    out = jax.ShapeDtypeStruct((t, LANES), F32)
    spec = pl.BlockSpec((tm, LANES), lambda i: (i, 0))
    return pl.pallas_call(
        _rope_table_body,
        grid=(t // tm,),
        in_specs=[pl.BlockSpec((tm, 1), lambda i: (i, 0)), pl.BlockSpec((1, LANES), lambda i: (0, 0))],
        out_specs=[spec, spec, spec],
        out_shape=[out, out, out],
        compiler_params=_cparams(("parallel",)),
        name="rope_tables",
    )(positions.reshape(t, 1), freq)


def _qkv_epilogue(accs, biases, acc2, extras):
    y = accs[0] + biases[0]
    c, s1, s2, m = extras
    half = ROT_DIM // 2
    outs = []
    for ci in range(y.shape[1] // LANES):
        xc = y[:, ci * LANES:(ci + 1) * LANES]
        mc = m[:, ci * LANES:(ci + 1) * LANES]
        rot = xc * c + pltpu.roll(xc, LANES - half, 1) * s1 + pltpu.roll(xc, half, 1) * s2
        outs.append(xc + mc * (rot - xc))
    return jnp.concatenate(outs, axis=1)


def _attn_body(sink_ref, q_ref, kp_ref, kc_ref, vp_ref, vc_ref, o_ref, *, n_kv, group):
    n = pl.program_id(1)
    blk = ATTN_BLOCK
    rows = group * blk
    qi = lax.broadcasted_iota(jnp.int32, (rows, 2 * blk), 0) % blk
    ki = lax.broadcasted_iota(jnp.int32, (rows, 2 * blk), 1)
    rel = qi + blk - ki
    mask = (rel >= 0) & (rel < WINDOW) & ((n > 0) | (ki >= blk))
    for h in range(n_kv):
        cs = slice(h * HEAD_DIM, (h + 1) * HEAD_DIM)
        kb = jnp.concatenate([kp_ref[:, cs], kc_ref[:, cs]], axis=0)
        vb = jnp.concatenate([vp_ref[:, cs], vc_ref[:, cs]], axis=0)
        q = jnp.concatenate(
            [q_ref[:, (h * group + g) * HEAD_DIM:(h * group + g + 1) * HEAD_DIM] for g in range(group)],
            axis=0)
        sink = jnp.concatenate(
            [jnp.full((blk, 1), sink_ref[h * group + g], F32) for g in range(group)], axis=0)
        s = lax.dot_general(q, kb, (((1,), (1,)), ((), ())), preferred_element_type=F32)
        s = jnp.where(mask, s * (HEAD_DIM ** -0.5), -jnp.inf)
        mx = jnp.maximum(jnp.max(s, axis=-1, keepdims=True), sink)
        e = jnp.exp(s - mx)
        den = jnp.sum(e, axis=-1, keepdims=True) + jnp.exp(sink - mx)
        o = jnp.dot(e.astype(BF16), vb, preferred_element_type=F32) / den
        for g in range(group):
            hq = h * group + g
            o_ref[:, hq * HEAD_DIM:(hq + 1) * HEAD_DIM] = o[g * blk:(g + 1) * blk].astype(o_ref.dtype)


def attention(qkv, sinks, bsz, seq):
    t, width = qkv.shape
    n_q = sinks.shape[0]
    n_kv = n_q // KV_RATIO if n_q >= KV_RATIO else 1
    group = n_q // n_kv
    dq = n_q * HEAD_DIM
    dkv = n_kv * HEAD_DIM
    assert width == dq + 2 * dkv and dq % dkv == 0
    nb = seq // ATTN_BLOCK
    kblk = dq // dkv
    cur = lambda b, n, s: (b * nb + n)
    prev = lambda b, n, s: (b * nb + jnp.maximum(n - 1, 0))
    grid_spec = pltpu.PrefetchScalarGridSpec(
        num_scalar_prefetch=1,
        grid=(bsz, nb),
        in_specs=[
            pl.BlockSpec((ATTN_BLOCK, dq), lambda b, n, s: (cur(b, n, s), 0)),
            pl.BlockSpec((ATTN_BLOCK, dkv), lambda b, n, s: (prev(b, n, s), kblk)),
            pl.BlockSpec((ATTN_BLOCK, dkv), lambda b, n, s: (cur(b, n, s), kblk)),
            pl.BlockSpec((ATTN_BLOCK, dkv), lambda b, n, s: (prev(b, n, s), kblk + 1)),
            pl.BlockSpec((ATTN_BLOCK, dkv), lambda b, n, s: (cur(b, n, s), kblk + 1)),
        ],
        out_specs=pl.BlockSpec((ATTN_BLOCK, dq), lambda b, n, s: (cur(b, n, s), 0)),
    )
    return pl.pallas_call(
        functools.partial(_attn_body, n_kv=n_kv, group=group),
        grid_spec=grid_spec,
        out_shape=jax.ShapeDtypeStruct((t, dq), BF16),
        compiler_params=_cparams(("parallel", "arbitrary")),
        name="swa_attention",
    )(sinks.astype(F32), qkv, qkv, qkv, qkv, qkv)


def _norm_body(x_ref, g_ref, o_ref):
    x = x_ref[...].astype(F32)
    r = lax.rsqrt(jnp.mean(x * x, axis=-1, keepdims=True) + RMS_EPS)
    o_ref[...] = ((x * r) * g_ref[...]).astype(o_ref.dtype)


def rms_norm(x, g, name):
    t, d = x.shape
    tm = min(1024, t)
    return pl.pallas_call(
        _norm_body,
        grid=(t // tm,),
        in_specs=[pl.BlockSpec((tm, d), lambda i: (i, 0)), pl.BlockSpec((1, d), lambda i: (0, 0))],
        out_specs=pl.BlockSpec((tm, d), lambda i: (i, 0)),
        out_shape=jax.ShapeDtypeStruct((t, d), F32),
        compiler_params=_cparams(("parallel",)),
        name=name,
    )(x, g.reshape(1, d).astype(F32))


def _s5_body(u_ref, win_ref, tp_ref, wout_ref, pwr_ref, pwi_ref, d_ref, o_ref, u2_ref, *, levels):
    L = S5_CHUNK
    rows = u_ref.shape[0] // L
    for s in range(L):
        u2_ref[:, s * LANES:(s + 1) * LANES] = u_ref[pl.ds(s, rows, stride=L), :].astype(BF16)
    u2 = u2_ref[...]
    xin = jnp.dot(u2, win_ref[0], preferred_element_type=F32)
    hw = xin.shape[1] // 2
    hr, hi = xin[:, :hw], xin[:, hw:]
    row = lax.broadcasted_iota(jnp.int32, (rows, hw), 0)
    for lv in range(levels):
        d = 1 << lv
        ar = pwr_ref[0, lv:lv + 1, :]
        ai = pwi_ref[0, lv:lv + 1, :]
        keep = row >= d
        sr = jnp.where(keep, pltpu.roll(hr, d, 0), 0.0)
        si = jnp.where(keep, pltpu.roll(hi, d, 0), 0.0)
        hr, hi = hr + (ar * sr - ai * si), hi + (ar * si + ai * sr)
    keep = row >= 1
    pr = jnp.where(keep, pltpu.roll(hr, 1, 0), 0.0)
    pi_ = jnp.where(keep, pltpu.roll(hi, 1, 0), 0.0)
    hprev = jnp.concatenate([pr, pi_], axis=1).astype(BF16)
    y = (jnp.dot(u2, tp_ref[0], preferred_element_type=F32)
         + jnp.dot(hprev, wout_ref[0], preferred_element_type=F32))
    dsk = d_ref[...]
    for t in range(L):
        yt = y[:, t * LANES:(t + 1) * LANES] + dsk * u_ref[pl.ds(t, rows, stride=L), :]
        o_ref[pl.ds(t, rows, stride=L), :] = 0.5 * yt * (1.0 + lax.erf(yt * (2.0 ** -0.5)))


def _s5_weights(a_re, a_im, log_dt, b_re, b_im, c_re, c_im, levels):
    hp = lax.Precision.HIGHEST
    L = S5_CHUNK
    g, n = a_re.shape
    nc = g * S5_GROUP // LANES
    ga = LANES // S5_GROUP
    dt = jnp.exp(log_dt.astype(F32))[:, None]
    ar = a_re.astype(F32)
    ai = a_im.astype(F32)
    mag = jnp.exp(ar * dt)
    lr = mag * jnp.cos(ai * dt)
    li = mag * jnp.sin(ai * dt)
    den = ar * ar + ai * ai
    zr = ((lr - 1.0) * ar + li * ai) / den
    zi = (li * ar - (lr - 1.0) * ai) / den
    br = b_re.astype(F32)
    bi = b_im.astype(F32)
    bbr = zr[..., None] * br - zi[..., None] * bi
    bbi = zr[..., None] * bi + zi[..., None] * br
    cr = c_re.astype(F32)
    ci = c_im.astype(F32)

    def lam_pow(e):
        e = e.astype(F32)[:, None, None]
        m_ = jnp.exp(e * (ar * dt))
        return m_ * jnp.cos(e * (ai * dt)), m_ * jnp.sin(e * (ai * dt))

    pr, pi_ = lam_pow(jnp.arange(L + 1))
    gr = cr[None] * pr[:L, :, None, :] - ci[None] * pi_[:L, :, None, :]
    gi = cr[None] * pi_[:L, :, None, :] + ci[None] * pr[:L, :, None, :]
    kmat = (jnp.einsum("tgon,gni->tgoi", gr, bbr, precision=hp)
            - jnp.einsum("tgon,gni->tgoi", gi, bbi, precision=hp))
    s_idx = jnp.arange(L)[:, None]
    t_idx = jnp.arange(L)[None, :]
    lag = t_idx - s_idx
    tp = jnp.where((lag >= 0)[:, :, None, None, None], kmat[jnp.clip(lag, 0, L - 1)], 0.0)
    eye = jnp.eye(ga, dtype=F32)
    tp = tp.reshape(L, L, nc, ga, S5_GROUP, S5_GROUP)
    tp_c = jnp.einsum("stcaoi,ab->csaitbo", tp, eye).reshape(nc, L * LANES, L * LANES)
    qr, qi = pr[L - 1 - jnp.arange(L)], pi_[L - 1 - jnp.arange(L)]
    wr = qr[..., None] * bbr[None] - qi[..., None] * bbi[None]
    wi = qr[..., None] * bbi[None] + qi[..., None] * bbr[None]
    win = jnp.stack([wr, wi]).reshape(2, L, nc, ga, n, S5_GROUP)
    win_c = jnp.einsum("rscani,ab->csairbn", win, eye).reshape(nc, L * LANES, 2 * ga * n)
    er, ei = pr[1:], pi_[1:]
    g_r = cr[None] * er[:, :, None, :] - ci[None] * ei[:, :, None, :]
    g_i = cr[None] * ei[:, :, None, :] + ci[None] * er[:, :, None, :]
    wo = jnp.stack([g_r, -g_i]).reshape(2, L, nc, ga, S5_GROUP, n)
    wout_c = jnp.einsum("rtcaon,ab->crantbo", wo, eye).reshape(nc, 2 * ga * n, L * LANES)
    sr, si = lam_pow(L * (2 ** jnp.arange(levels)))
    pwr = sr.reshape(levels, nc, ga * n).transpose(1, 0, 2)
    pwi = si.reshape(levels, nc, ga * n).transpose(1, 0, 2)
    return win_c.astype(BF16), tp_c.astype(BF16), wout_c.astype(BF16), pwr, pwi


def s5_mix(u, weights, d_skip, bsz, seq):
    t, d = u.shape
    win_c, tp_c, wout_c, pwr, pwi = weights
    nc = d // LANES
    levels = pwr.shape[1]
    L = S5_CHUNK
    rows = seq // L
    wspec = lambda a: pl.BlockSpec((1,) + a.shape[1:], lambda c, b: (c, 0, 0))
    return pl.pallas_call(
        functools.partial(_s5_body, levels=levels),
        grid=(nc, bsz),
        in_specs=[
            pl.BlockSpec((seq, LANES), lambda c, b: (b, c)),
            wspec(win_c), wspec(tp_c), wspec(wout_c), wspec(pwr), wspec(pwi),
            pl.BlockSpec((1, LANES), lambda c, b: (0, c)),
        ],
        out_specs=pl.BlockSpec((seq, LANES), lambda c, b: (b, c)),
        out_shape=jax.ShapeDtypeStruct((t, d), F32),
        scratch_shapes=[pltpu.VMEM((rows, L * LANES), BF16)],
        compiler_params=_cparams(("parallel", "arbitrary")),
        name="s5_chunked",
    )(u, win_c, tp_c, wout_c, pwr, pwi, d_skip.reshape(1, d).astype(F32))


def _router_epilogue(accs, biases, acc2, extras):
    logits = accs[0] + biases[0]
    lane = lax.broadcasted_iota(jnp.int32, logits.shape, 1)
    big = jnp.int32(logits.shape[1])
    m1 = jnp.max(logits, axis=-1, keepdims=True)
    i1 = jnp.min(jnp.where(logits == m1, lane, big), axis=-1, keepdims=True)
    rest = jnp.where(lane == i1, -jnp.inf, logits)
    m2 = jnp.max(rest, axis=-1, keepdims=True)
    i2 = jnp.min(jnp.where(rest == m2, lane, big), axis=-1, keepdims=True)
    e2 = jnp.exp(m2 - m1)
    den = 1.0 + e2
    return jnp.where(lane == i1, 1.0 / den, 0.0) + jnp.where(lane == i2, e2 / den, 0.0)


def kernel(x, p, positions, norm_mix, norm_ffn, norm_ple, norm_final, attn_w_qkv, attn_b_qkv, attn_w_o, attn_b_o, attn_sinks, s5_a_re, s5_a_im, s5_log_dt, s5_b_re, s5_b_im, s5_c_re, s5_c_im, s5_d, s5_w_glu, s5_b_glu, ffn_w_gate_up, ffn_w_down, moe_w_router, moe_b_router, moe_w_gate_up, moe_w_down, ple_w_proj, ple_w_gate):
    bsz, seq, d = x.shape
    depth = p.shape[0]
    t = bsz * seq
    d_ff = ffn_w_down.shape[1]
    n_exp = moe_w_router.shape[-1]
    qkv_w = attn_w_qkv.shape[-1]
    dq = attn_w_o.shape[1]
    tm = 1024

    h = x.reshape(t, d).astype(F32)
    rope_c, rope_s1, rope_s2 = rope_tables(positions)
    rope_cols = (jnp.arange(qkv_w) < dq + (qkv_w - dq) // 2).astype(F32).reshape(1, qkv_w)
    levels = max(1, int(math.ceil(math.log2(seq // S5_CHUNK))))

    def resid_add(accs, biases, acc2, extras):
        return extras[0] + accs[0]

    def swiglu_up(accs, biases, acc2, extras):
        return jax.nn.silu(accs[0]) * accs[1]

    def ple_ep(accs, biases, acc2, extras):
        return extras[0] + jax.nn.sigmoid(accs[0]) * acc2

    for i in range(depth):
        k = i // 2
        if i % 2 == 0:
            tn = 512
            qkv = fused_mm(
                h, [(attn_w_qkv[k].astype(BF16), 0)], tm=tm, tn=tn, n_out=qkv_w,
                gain=norm_mix[i], biases=[(attn_b_qkv[k], 0)],
                extras=[(rope_c, "rows"), (rope_s1, "rows"), (rope_s2, "rows"), (rope_cols, "cols")],
                epilogue=_qkv_epilogue, out_dtype=BF16, name="qkv_rope")
            o = attention(qkv, attn_sinks[k], bsz, seq)
            h = fused_mm(
                o, [(attn_w_o[k].astype(BF16), 0)], tm=tm, tn=1024, n_out=d,
                biases=[(attn_b_o[k], 0)], extras=[(h, "tile")],
                epilogue=lambda accs, biases, acc2, extras: extras[0] + (accs[0] + biases[0]),
                out_dtype=F32, name="attn_out")
            wgu = ffn_w_gate_up[k].astype(BF16)
            a = fused_mm(h, [(wgu, 0), (wgu, d_ff)], tm=tm, tn=512, n_out=d_ff, gain=norm_ffn[i],
                         epilogue=swiglu_up, out_dtype=BF16, name="ffn_up")
            h = fused_mm(a, [(ffn_w_down[k].astype(BF16), 0)], tm=tm, tn=1024, n_out=d, tk=1024,
                         extras=[(h, "tile")], epilogue=resid_add, out_dtype=F32,
                         name="ffn_down")
        else:
            u = rms_norm(h, norm_mix[i], "s5_norm")
            weights = _s5_weights(s5_a_re[k], s5_a_im[k], s5_log_dt[k], s5_b_re[k], s5_b_im[k],
                                  s5_c_re[k], s5_c_im[k], levels)
            y = s5_mix(u, weights, s5_d[k], bsz, seq)
            wglu = s5_w_glu[k].astype(BF16)
            h = fused_mm(
                y, [(wglu, 0), (wglu, d)], tm=tm, tn=512, n_out=d,
                biases=[(s5_b_glu[k], 0), (s5_b_glu[k], d)], extras=[(h, "tile")],
                epilogue=lambda accs, biases, acc2, extras: extras[0] + (accs[0] + biases[0]) * jax.nn.sigmoid(accs[1] + biases[1]),
                out_dtype=F32, name="s5_glu")
            w_r = jnp.zeros((d, LANES), F32).at[:, :n_exp].set(moe_w_router[k].astype(F32))
            b_r = jnp.full((LANES,), -jnp.inf, F32).at[:n_exp].set(moe_b_router[k].astype(F32))
            gates = fused_mm(h, [(w_r.astype(BF16), 0)], tm=tm, tn=LANES, n_out=LANES, gain=norm_ffn[i],
                             biases=[(b_r, 0)], epilogue=_router_epilogue, out_dtype=F32, name="router")
            h_in = h
            for e in range(n_exp):
                wgu = moe_w_gate_up[k, e].astype(BF16)
                a = fused_mm(
                    h_in, [(wgu, 0), (wgu, d_ff)], tm=tm, tn=512, n_out=d_ff,
                    gain=norm_ffn[i], extras=[(gates, "rows")],
                    epilogue=lambda accs, biases, acc2, extras, e=e: extras[0][:, e:e + 1] * (jax.nn.silu(accs[0]) * accs[1]),
                    out_dtype=BF16, name="moe_up")
                h = fused_mm(a, [(moe_w_down[k, e].astype(BF16), 0)], tm=tm, tn=1024, n_out=d, tk=1024,
                             extras=[(h, "tile")], epilogue=resid_add, out_dtype=F32,
                             name="moe_down")
        h = fused_mm(
            h, [(ple_w_gate[i].astype(BF16), 0)], tm=tm, tn=512, n_out=d, gain=norm_ple[i],
            pair2=(p[i].reshape(t, -1).astype(BF16), ple_w_proj[i].astype(BF16)),
            extras=[(h, "tile")], epilogue=ple_ep, out_dtype=F32, name="ple")
    out = rms_norm(h, norm_final, "final_norm")
    return out.reshape(bsz, seq, d).astype(x.dtype)
```

```python
import functools
import math

import jax
import jax.numpy as jnp
from jax import lax
from jax.experimental import pallas as pl
from jax.experimental.pallas import tpu as pltpu

F32 = jnp.float32
BF16 = jnp.bfloat16

HEAD_DIM = 64
KV_RATIO = 8
WINDOW = 128
ATTN_BLOCK = 128
ROT_DIM = HEAD_DIM // 4
ROPE_THETA = 500000.0
S5_GROUP = 16
S5_CHUNK = 8
RMS_EPS = 1e-6
LANES = 128
MOE_TM = 512
GATHER_ROWS = 256
VMEM_LIMIT = 60 * 1024 * 1024


def _cparams(sem):
    return pltpu.CompilerParams(dimension_semantics=sem, vmem_limit_bytes=VMEM_LIMIT)


def _fit(tile, n):
    if n <= tile:
        return n
    return max(c for c in range(LANES, tile + 1, LANES) if n % c == 0)


def _mm_body(*refs, norm, stage, grouped, n_rhs, n_bias, pair2, n_extra, nk, epilogue):
    it = iter(refs)
    nt_ref = None
    if grouped:
        next(it)
        nt_ref = next(it)
    x_ref = next(it)
    g_ref = next(it) if norm else None
    w_refs = [next(it) for _ in range(n_rhs)]
    b_refs = [next(it) for _ in range(n_bias)]
    x2_ref, w2_ref = (next(it), next(it)) if pair2 else (None, None)
    e_refs = [next(it) for _ in range(n_extra)]
    o_ref = next(it)
    xs_ref = next(it) if stage else None
    acc_refs = [next(it) for _ in range(n_rhs)] if nk > 1 else []

    i = pl.program_id(0)
    j = pl.program_id(1)
    k = pl.program_id(2)

    def compute():
        if stage:
            @pl.when(j == 0)
            def _():
                x = x_ref[...].astype(F32)
                if norm:
                    r = lax.rsqrt(jnp.mean(x * x, axis=-1, keepdims=True) + RMS_EPS)
                    x = (x * r) * g_ref[...]
                xs_ref[...] = x.astype(BF16)
            lhs = xs_ref[...]
        else:
            lhs = x_ref[...].astype(BF16)

        parts = [jnp.dot(lhs, w[...], preferred_element_type=F32) for w in w_refs]

        def finish(accs):
            biases = [b[...] for b in b_refs]
            acc2 = None
            if pair2:
                acc2 = jnp.dot(x2_ref[...].astype(BF16), w2_ref[...], preferred_element_type=F32)
            o_ref[...] = epilogue(accs, biases, acc2, [e[...] for e in e_refs]).astype(o_ref.dtype)

        if nk == 1:
            finish(parts)
        else:
            @pl.when(k == 0)
            def _():
                for a, p_ in zip(acc_refs, parts):
                    a[...] = p_

            @pl.when(k > 0)
            def _():
                for a, p_ in zip(acc_refs, parts):
                    a[...] += p_

            @pl.when(k == nk - 1)
            def _():
                finish([a[...] for a in acc_refs])

    if grouped:
        live = i < nt_ref[0]
        pl.when(live)(compute)

        @pl.when(jnp.logical_not(live) & (k == nk - 1))
        def _():
            o_ref[...] = jnp.zeros(o_ref.shape, o_ref.dtype)
    else:
        compute()


def fused_mm(x, ws, *, tm, tn, n_out, epilogue, out_dtype, gain=None, biases=(),
             pair2=None, extras=(), tk=None, group=None, name=None):
    m, kdim = x.shape
    tm = _fit(tm, m)
    tn = _fit(tn, n_out)
    tk = kdim if tk is None else _fit(tk, kdim)
    norm = gain is not None
    assert m % tm == 0 and n_out % tn == 0 and kdim % tk == 0
    nk = kdim // tk
    nj = n_out // tn
    assert not (norm and nk > 1)
    stage = norm or (nk == 1 and x.dtype != BF16)
    grouped = group is not None
    grid = (m // tm, nj, nk)

    if grouped:
        def ijk(i, j, k, pf):
            live = i < pf[1][0]
            return (jnp.minimum(i, pf[1][0] - 1), jnp.where(live, j, nj - 1), jnp.where(live, k, nk - 1))
    else:
        def ijk(i, j, k, pf):
            return i, j, k

    def x_map(i, j, k, *pf):
        i, j, k = ijk(i, j, k, pf)
        return (i, k)

    def w_map(ob):
        def f(i, j, k, *pf):
            _, jj, kk = ijk(i, j, k, pf)
            return (pf[0][i], kk, jj + ob) if grouped else (kk, jj + ob)
        return f

    def col_map(ob):
        def f(i, j, k, *pf):
            i, j, k = ijk(i, j, k, pf)
            return (0, j + ob)
        return f

    def row_map(i, j, k, *pf):
        i, j, k = ijk(i, j, k, pf)
        return (i, 0)

    def tile_map(i, j, k, *pf):
        i, j, k = ijk(i, j, k, pf)
        return (i, j)

    in_specs = [pl.BlockSpec((tm, tk), x_map)]
    args = [x]
    if norm:
        in_specs.append(pl.BlockSpec((1, kdim), lambda i, j, k, *pf: (0, 0)))
        args.append(gain.reshape(1, kdim).astype(F32))
    for w, off in ws:
        assert off % tn == 0
        wshape = (None, tk, tn) if grouped else (tk, tn)
        in_specs.append(pl.BlockSpec(wshape, w_map(off // tn)))
        args.append(w)
    for b, off in biases:
        in_specs.append(pl.BlockSpec((1, tn), col_map(off // tn)))
        args.append(b.reshape(1, -1).astype(F32))
    if pair2 is not None:
        x2, w2 = pair2
        k2 = x2.shape[1]
        in_specs.append(pl.BlockSpec((tm, k2), row_map))
        in_specs.append(pl.BlockSpec((k2, tn), col_map(0)))
        args += [x2, w2]
    for arr, kind in extras:
        if kind == "tile":
            in_specs.append(pl.BlockSpec((tm, tn), tile_map))
        elif kind == "rows":
            in_specs.append(pl.BlockSpec((tm, arr.shape[1]), row_map))
        else:
            in_specs.append(pl.BlockSpec((1, tn), col_map(0)))
        args.append(arr)

    scratch = []
    if stage:
        scratch.append(pltpu.VMEM((tm, kdim), BF16))
    if nk > 1:
        scratch += [pltpu.VMEM((tm, tn), F32) for _ in ws]

    body = functools.partial(_mm_body, norm=norm, stage=stage, grouped=grouped, n_rhs=len(ws),
                             n_bias=len(biases), pair2=pair2 is not None, n_extra=len(extras),
                             nk=nk, epilogue=epilogue)
    grid_spec = pltpu.PrefetchScalarGridSpec(
        num_scalar_prefetch=2 if grouped else 0,
        grid=grid,
        in_specs=in_specs,
        out_specs=pl.BlockSpec((tm, tn), lambda i, j, k, *pf: (i, j)),
        scratch_shapes=scratch,
    )
    return pl.pallas_call(
        body,
        grid_spec=grid_spec,
        out_shape=jax.ShapeDtypeStruct((m, n_out), out_dtype),
        compiler_params=_cparams(("parallel", "arbitrary", "arbitrary")),
        name=name,
    )(*(tuple(group) if grouped else ()), *args)


def _rope_table_body(pos_ref, freq_ref, c_ref, s1_ref, s2_ref):
    ang = pos_ref[...].astype(F32) * freq_ref[...]
    lane = lax.broadcasted_iota(jnp.int32, ang.shape, 1) % HEAD_DIM
    cos = jnp.cos(ang)
    sin = jnp.sin(ang)
    half = ROT_DIM // 2
    c_ref[...] = jnp.where(lane < ROT_DIM, cos, 1.0)
    s1_ref[...] = jnp.where(lane < half, -sin, 0.0)
    s2_ref[...] = jnp.where((lane >= half) & (lane < ROT_DIM), sin, 0.0)


def rope_tables(positions):
    t = positions.size
    tm = min(1024, t)
    inv_freq = jnp.power(ROPE_THETA, -jnp.arange(0, ROT_DIM, 2, dtype=F32) / ROT_DIM)
    lane = jnp.arange(LANES) % HEAD_DIM
    freq = jnp.where(lane < ROT_DIM, inv_freq[lane % (ROT_DIM // 2)], 0.0).reshape(1, LANES)
    out = jax.ShapeDtypeStruct((t, LANES), F32)
    spec = pl.BlockSpec((tm, LANES), lambda i: (i, 0))
    return pl.pallas_call(
        _rope_table_body,
        grid=(t // tm,),
        in_specs=[pl.BlockSpec((tm, 1), lambda i: (i, 0)), pl.BlockSpec((1, LANES), lambda i: (0, 0))],
        out_specs=[spec, spec, spec],
        out_shape=[out, out, out],
        compiler_params=_cparams(("parallel",)),
        name="rope_tables",
    )(positions.reshape(t, 1), freq)


def _qkv_epilogue(accs, biases, acc2, extras):
    y = accs[0] + biases[0]
    c, s1, s2, m = extras
    half = ROT_DIM // 2
    outs = []
    for ci in range(y.shape[1] // LANES):
        xc = y[:, ci * LANES:(ci + 1) * LANES]
        mc = m[:, ci * LANES:(ci + 1) * LANES]
        rot = xc * c + pltpu.roll(xc, LANES - half, 1) * s1 + pltpu.roll(xc, half, 1) * s2
        outs.append(xc + mc * (rot - xc))
    return jnp.concatenate(outs, axis=1)


def _attn_body(sink_ref, q_ref, kp_ref, kc_ref, vp_ref, vc_ref, o_ref, *, n_kv, group):
    n = pl.program_id(1)
    blk = ATTN_BLOCK
    rows = group * blk
    qi = lax.broadcasted_iota(jnp.int32, (rows, 2 * blk), 0) % blk
    ki = lax.broadcasted_iota(jnp.int32, (rows, 2 * blk), 1)
    rel = qi + blk - ki
    mask = (rel >= 0) & (rel < WINDOW) & ((n > 0) | (ki >= blk))
    for h in range(n_kv):
        cs = slice(h * HEAD_DIM, (h + 1) * HEAD_DIM)
        kb = jnp.concatenate([kp_ref[:, cs], kc_ref[:, cs]], axis=0)
        vb = jnp.concatenate([vp_ref[:, cs], vc_ref[:, cs]], axis=0)
        q = jnp.concatenate(
            [q_ref[:, (h * group + g) * HEAD_DIM:(h * group + g + 1) * HEAD_DIM] for g in range(group)],
            axis=0)
        sink = jnp.concatenate(
            [jnp.full((blk, 1), sink_ref[h * group + g], F32) for g in range(group)], axis=0)
        s = lax.dot_general(q, kb, (((1,), (1,)), ((), ())), preferred_element_type=F32)
        s = jnp.where(mask, s * (HEAD_DIM ** -0.5), -jnp.inf)
        mx = jnp.maximum(jnp.max(s, axis=-1, keepdims=True), sink)
        e = jnp.exp(s - mx)
        den = jnp.sum(e, axis=-1, keepdims=True) + jnp.exp(sink - mx)
        o = jnp.dot(e.astype(BF16), vb, preferred_element_type=F32) / den
        for g in range(group):
            hq = h * group + g
            o_ref[:, hq * HEAD_DIM:(hq + 1) * HEAD_DIM] = o[g * blk:(g + 1) * blk].astype(o_ref.dtype)


def attention(qkv, sinks, bsz, seq):
    t, width = qkv.shape
    n_q = sinks.shape[0]
    n_kv = n_q // KV_RATIO if n_q >= KV_RATIO else 1
    group = n_q // n_kv
    dq = n_q * HEAD_DIM
    dkv = n_kv * HEAD_DIM
    assert width == dq + 2 * dkv and dq % dkv == 0
    nb = seq // ATTN_BLOCK
    kblk = dq // dkv
    cur = lambda b, n, s: (b * nb + n)
    prev = lambda b, n, s: (b * nb + jnp.maximum(n - 1, 0))
    grid_spec = pltpu.PrefetchScalarGridSpec(
        num_scalar_prefetch=1,
        grid=(bsz, nb),
        in_specs=[
            pl.BlockSpec((ATTN_BLOCK, dq), lambda b, n, s: (cur(b, n, s), 0)),
            pl.BlockSpec((ATTN_BLOCK, dkv), lambda b, n, s: (prev(b, n, s), kblk)),
            pl.BlockSpec((ATTN_BLOCK, dkv), lambda b, n, s: (cur(b, n, s), kblk)),
            pl.BlockSpec((ATTN_BLOCK, dkv), lambda b, n, s: (prev(b, n, s), kblk + 1)),
            pl.BlockSpec((ATTN_BLOCK, dkv), lambda b, n, s: (cur(b, n, s), kblk + 1)),
        ],
        out_specs=pl.BlockSpec((ATTN_BLOCK, dq), lambda b, n, s: (cur(b, n, s), 0)),
    )
    return pl.pallas_call(
        functools.partial(_attn_body, n_kv=n_kv, group=group),
        grid_spec=grid_spec,
        out_shape=jax.ShapeDtypeStruct((t, dq), BF16),
        compiler_params=_cparams(("parallel", "arbitrary")),
        name="swa_attention",
    )(sinks.astype(F32), qkv, qkv, qkv, qkv, qkv)


def _norm_body(x_ref, g_ref, o_ref):
    x = x_ref[...].astype(F32)
    r = lax.rsqrt(jnp.mean(x * x, axis=-1, keepdims=True) + RMS_EPS)
    o_ref[...] = ((x * r) * g_ref[...]).astype(o_ref.dtype)


def rms_norm(x, g, name):
    t, d = x.shape
    tm = min(1024, t)
    return pl.pallas_call(
        _norm_body,
        grid=(t // tm,),
        in_specs=[pl.BlockSpec((tm, d), lambda i: (i, 0)), pl.BlockSpec((1, d), lambda i: (0, 0))],
        out_specs=pl.BlockSpec((tm, d), lambda i: (i, 0)),
        out_shape=jax.ShapeDtypeStruct((t, d), F32),
        compiler_params=_cparams(("parallel",)),
        name=name,
    )(x, g.reshape(1, d).astype(F32))


def _s5_body(u_ref, win_ref, tp_ref, wout_ref, pwr_ref, pwi_ref, d_ref, o_ref, u2_ref, *, levels):
    L = S5_CHUNK
    rows = u_ref.shape[0] // L
    for s in range(L):
        u2_ref[:, s * LANES:(s + 1) * LANES] = u_ref[pl.ds(s, rows, stride=L), :].astype(BF16)
    u2 = u2_ref[...]
    xin = jnp.dot(u2, win_ref[0], preferred_element_type=F32)
    hw = xin.shape[1] // 2
    hr, hi = xin[:, :hw], xin[:, hw:]
    row = lax.broadcasted_iota(jnp.int32, (rows, hw), 0)
    for lv in range(levels):
        d = 1 << lv
        ar = pwr_ref[0, lv:lv + 1, :]
        ai = pwi_ref[0, lv:lv + 1, :]
        keep = row >= d
        sr = jnp.where(keep, pltpu.roll(hr, d, 0), 0.0)
        si = jnp.where(keep, pltpu.roll(hi, d, 0), 0.0)
        hr, hi = hr + (ar * sr - ai * si), hi + (ar * si + ai * sr)
    keep = row >= 1
    pr = jnp.where(keep, pltpu.roll(hr, 1, 0), 0.0)
    pi_ = jnp.where(keep, pltpu.roll(hi, 1, 0), 0.0)
    hprev = jnp.concatenate([pr, pi_], axis=1).astype(BF16)
    y = (jnp.dot(u2, tp_ref[0], preferred_element_type=F32)
         + jnp.dot(hprev, wout_ref[0], preferred_element_type=F32))
    dsk = d_ref[...]
    for t in range(L):
        yt = y[:, t * LANES:(t + 1) * LANES] + dsk * u_ref[pl.ds(t, rows, stride=L), :]
        o_ref[pl.ds(t, rows, stride=L), :] = 0.5 * yt * (1.0 + lax.erf(yt * (2.0 ** -0.5)))


def _s5_weights(a_re, a_im, log_dt, b_re, b_im, c_re, c_im, levels):
    hp = lax.Precision.HIGHEST
    L = S5_CHUNK
    g, n = a_re.shape
    nc = g * S5_GROUP // LANES
    ga = LANES // S5_GROUP
    dt = jnp.exp(log_dt.astype(F32))[:, None]
    ar = a_re.astype(F32)
    ai = a_im.astype(F32)
    mag = jnp.exp(ar * dt)
    lr = mag * jnp.cos(ai * dt)
    li = mag * jnp.sin(ai * dt)
    den = ar * ar + ai * ai
    zr = ((lr - 1.0) * ar + li * ai) / den
    zi = (li * ar - (lr - 1.0) * ai) / den
    br = b_re.astype(F32)
    bi = b_im.astype(F32)
    bbr = zr[..., None] * br - zi[..., None] * bi
    bbi = zr[..., None] * bi + zi[..., None] * br
    cr = c_re.astype(F32)
    ci = c_im.astype(F32)

    def lam_pow(e):
        e = e.astype(F32)[:, None, None]
        m_ = jnp.exp(e * (ar * dt))
        return m_ * jnp.cos(e * (ai * dt)), m_ * jnp.sin(e * (ai * dt))

    pr, pi_ = lam_pow(jnp.arange(L + 1))
    gr = cr[None] * pr[:L, :, None, :] - ci[None] * pi_[:L, :, None, :]
    gi = cr[None] * pi_[:L, :, None, :] + ci[None] * pr[:L, :, None, :]
    kmat = (jnp.einsum("tgon,gni->tgoi", gr, bbr, precision=hp)
            - jnp.einsum("tgon,gni->tgoi", gi, bbi, precision=hp))
    s_idx = jnp.arange(L)[:, None]
    t_idx = jnp.arange(L)[None, :]
    lag = t_idx - s_idx
    tp = jnp.where((lag >= 0)[:, :, None, None, None], kmat[jnp.clip(lag, 0, L - 1)], 0.0)
    eye = jnp.eye(ga, dtype=F32)
    tp = tp.reshape(L, L, nc, ga, S5_GROUP, S5_GROUP)
    tp_c = jnp.einsum("stcaoi,ab->csaitbo", tp, eye).reshape(nc, L * LANES, L * LANES)
    qr, qi = pr[L - 1 - jnp.arange(L)], pi_[L - 1 - jnp.arange(L)]
    wr = qr[..., None] * bbr[None] - qi[..., None] * bbi[None]
    wi = qr[..., None] * bbi[None] + qi[..., None] * bbr[None]
    win = jnp.stack([wr, wi]).reshape(2, L, nc, ga, n, S5_GROUP)
    win_c = jnp.einsum("rscani,ab->csairbn", win, eye).reshape(nc, L * LANES, 2 * ga * n)
    er, ei = pr[1:], pi_[1:]
    g_r = cr[None] * er[:, :, None, :] - ci[None] * ei[:, :, None, :]
    g_i = cr[None] * ei[:, :, None, :] + ci[None] * er[:, :, None, :]
    wo = jnp.stack([g_r, -g_i]).reshape(2, L, nc, ga, S5_GROUP, n)
    wout_c = jnp.einsum("rtcaon,ab->crantbo", wo, eye).reshape(nc, 2 * ga * n, L * LANES)
    sr, si = lam_pow(L * (2 ** jnp.arange(levels)))
    pwr = sr.reshape(levels, nc, ga * n).transpose(1, 0, 2)
    pwi = si.reshape(levels, nc, ga * n).transpose(1, 0, 2)
    return win_c.astype(BF16), tp_c.astype(BF16), wout_c.astype(BF16), pwr, pwi


def s5_mix(u, weights, d_skip, bsz, seq):
    t, d = u.shape
    win_c, tp_c, wout_c, pwr, pwi = weights
    nc = d // LANES
    levels = pwr.shape[1]
    L = S5_CHUNK
    rows = seq // L
    wspec = lambda a: pl.BlockSpec((1,) + a.shape[1:], lambda c, b: (c, 0, 0))
    return pl.pallas_call(
        functools.partial(_s5_body, levels=levels),
        grid=(nc, bsz),
        in_specs=[
            pl.BlockSpec((seq, LANES), lambda c, b: (b, c)),
            wspec(win_c), wspec(tp_c), wspec(wout_c), wspec(pwr), wspec(pwi),
            pl.BlockSpec((1, LANES), lambda c, b: (0, c)),
        ],
        out_specs=pl.BlockSpec((seq, LANES), lambda c, b: (b, c)),
        out_shape=jax.ShapeDtypeStruct((t, d), F32),
        scratch_shapes=[pltpu.VMEM((rows, L * LANES), BF16)],
        compiler_params=_cparams(("parallel", "arbitrary")),
        name="s5_chunked",
    )(u, win_c, tp_c, wout_c, pwr, pwi, d_skip.reshape(1, d).astype(F32))


def _router_epilogue(accs, biases, acc2, extras):
    logits = accs[0] + biases[0]
    lane = lax.broadcasted_iota(jnp.int32, logits.shape, 1)
    big = jnp.int32(logits.shape[1])
    m1 = jnp.max(logits, axis=-1, keepdims=True)
    i1 = jnp.min(jnp.where(logits == m1, lane, big), axis=-1, keepdims=True)
    rest = jnp.where(lane == i1, -jnp.inf, logits)
    m2 = jnp.max(rest, axis=-1, keepdims=True)
    i2 = jnp.min(jnp.where(rest == m2, lane, big), axis=-1, keepdims=True)
    e2 = jnp.exp(m2 - m1)
    den = 1.0 + e2
    rec = jnp.where(lane == 0, 1.0 / den, 0.0) + jnp.where(lane == 1, e2 / den, 0.0)
    rec = rec + jnp.where(lane == 2, i1.astype(F32), 0.0) + jnp.where(lane == 3, i2.astype(F32), 0.0)
    return rec


def _route_plan(route, n_exp, tm):
    t = route.shape[0]
    e_flat = route[:, 2:4].astype(jnp.int32).T.reshape(-1)
    w_flat = route[:, 0:2].T.reshape(-1)
    onehot = (e_flat[:, None] == jnp.arange(n_exp, dtype=jnp.int32)[None, :]).astype(jnp.int32)
    csum = jnp.cumsum(onehot, axis=0)
    rank = jnp.sum(csum * onehot, axis=1) - 1
    counts = csum[-1]
    padded = ((counts + tm - 1) // tm) * tm
    ends = jnp.cumsum(padded)
    starts = ends - padded
    pos = starts[e_flat] + rank
    n_tiles = (2 * t + n_exp * (tm - 1)) // tm
    n_live = (ends[-1] // tm).astype(jnp.int32)
    tile_expert = jnp.searchsorted(ends, jnp.arange(n_tiles, dtype=jnp.int32) * tm, side="right")
    tile_expert = jnp.minimum(tile_expert, tile_expert[n_live - 1]).astype(jnp.int32)
    rows = n_tiles * tm
    tok = jnp.tile(jnp.arange(t, dtype=jnp.int32), 2)
    row_token = jnp.zeros((rows,), jnp.int32).at[pos].set(tok, unique_indices=True)
    row_gate = jnp.zeros((rows,), F32).at[pos].set(w_flat, unique_indices=True)
    return row_token, row_gate.reshape(rows, 1), pos[:t], pos[t:], tile_expert, n_live.reshape(1)


def _gather_body(idx_ref, src_ref, o_ref, sem):
    n = o_ref.shape[0]

    def copy(r):
        return pltpu.make_async_copy(src_ref.at[pl.ds(idx_ref[0, 0, r], 1)], o_ref.at[pl.ds(r, 1)], sem)

    def start(r, c):
        copy(r).start()
        return c

    def wait(r, c):
        copy(r).wait()
        return c

    lax.fori_loop(0, n, start, 0, unroll=8)
    lax.fori_loop(0, n, wait, 0, unroll=8)


def gather_rows(src, row_idx):
    rows = row_idx.shape[0]
    d = src.shape[1]
    tg = _fit(GATHER_ROWS, rows)
    return pl.pallas_call(
        _gather_body,
        grid=(rows // tg,),
        in_specs=[
            pl.BlockSpec((1, 1, tg), lambda i: (i, 0, 0), memory_space=pltpu.SMEM),
            pl.BlockSpec(memory_space=pl.ANY),
        ],
        out_specs=pl.BlockSpec((tg, d), lambda i: (i, 0)),
        out_shape=jax.ShapeDtypeStruct((rows, d), src.dtype),
        scratch_shapes=[pltpu.SemaphoreType.DMA(())],
        compiler_params=_cparams(("arbitrary",)),
        name="moe_gather",
    )(row_idx.reshape(rows // tg, 1, tg), src)


def _combine_body(i1_ref, i2_ref, ys_ref, h_ref, o_ref, buf_ref, sem):
    n = o_ref.shape[0]

    def copies(r):
        return (pltpu.make_async_copy(ys_ref.at[pl.ds(i1_ref[0, 0, r], 1)], buf_ref.at[0, pl.ds(r, 1)], sem.at[0]),
                pltpu.make_async_copy(ys_ref.at[pl.ds(i2_ref[0, 0, r], 1)], buf_ref.at[1, pl.ds(r, 1)], sem.at[1]))

    def start(r, c):
        for cp in copies(r):
            cp.start()
        return c

    def wait(r, c):
        for cp in copies(r):
            cp.wait()
        return c

    lax.fori_loop(0, n, start, 0, unroll=8)
    lax.fori_loop(0, n, wait, 0, unroll=8)
    o_ref[...] = h_ref[...] + (buf_ref[0] + buf_ref[1])


def combine_rows(h, ys, pos1, pos2):
    t, d = h.shape
    tg = _fit(GATHER_ROWS, t)
    ispec = pl.BlockSpec((1, 1, tg), lambda i: (i, 0, 0), memory_space=pltpu.SMEM)
    return pl.pallas_call(
        _combine_body,
        grid=(t // tg,),
        in_specs=[ispec, ispec, pl.BlockSpec(memory_space=pl.ANY), pl.BlockSpec((tg, d), lambda i: (i, 0))],
        out_specs=pl.BlockSpec((tg, d), lambda i: (i, 0)),
        out_shape=jax.ShapeDtypeStruct((t, d), h.dtype),
        scratch_shapes=[pltpu.VMEM((2, tg, d), ys.dtype), pltpu.SemaphoreType.DMA((2,))],
        compiler_params=_cparams(("arbitrary",)),
        name="moe_combine",
    )(pos1.reshape(t // tg, 1, tg), pos2.reshape(t // tg, 1, tg), ys, h)


def kernel(x, p, positions, norm_mix, norm_ffn, norm_ple, norm_final, attn_w_qkv, attn_b_qkv, attn_w_o, attn_b_o, attn_sinks, s5_a_re, s5_a_im, s5_log_dt, s5_b_re, s5_b_im, s5_c_re, s5_c_im, s5_d, s5_w_glu, s5_b_glu, ffn_w_gate_up, ffn_w_down, moe_w_router, moe_b_router, moe_w_gate_up, moe_w_down, ple_w_proj, ple_w_gate):
    bsz, seq, d = x.shape
    depth = p.shape[0]
    t = bsz * seq
    d_ff = ffn_w_down.shape[1]
    n_exp = moe_w_router.shape[-1]
    qkv_w = attn_w_qkv.shape[-1]
    dq = attn_w_o.shape[1]
    tm = 1024

    h = x.reshape(t, d).astype(F32)
    rope_c, rope_s1, rope_s2 = rope_tables(positions)
    rope_cols = (jnp.arange(qkv_w) < dq + (qkv_w - dq) // 2).astype(F32).reshape(1, qkv_w)
    levels = max(1, int(math.ceil(math.log2(seq // S5_CHUNK))))

    def resid_add(accs, biases, acc2, extras):
        return extras[0] + accs[0]

    def swiglu_up(accs, biases, acc2, extras):
        return jax.nn.silu(accs[0]) * accs[1]

    def gated_swiglu_up(accs, biases, acc2, extras):
        return extras[0] * (jax.nn.silu(accs[0]) * accs[1])

    def ple_ep(accs, biases, acc2, extras):
        return extras[0] + jax.nn.sigmoid(accs[0]) * acc2

    for i in range(depth):
        k = i // 2
        if i % 2 == 0:
            qkv = fused_mm(
                h, [(attn_w_qkv[k].astype(BF16), 0)], tm=tm, tn=512, n_out=qkv_w,
                gain=norm_mix[i], biases=[(attn_b_qkv[k], 0)],
                extras=[(rope_c, "rows"), (rope_s1, "rows"), (rope_s2, "rows"), (rope_cols, "cols")],
                epilogue=_qkv_epilogue, out_dtype=BF16, name="qkv_rope")
            o = attention(qkv, attn_sinks[k], bsz, seq)
            h = fused_mm(
                o, [(attn_w_o[k].astype(BF16), 0)], tm=tm, tn=1024, n_out=d,
                biases=[(attn_b_o[k], 0)], extras=[(h, "tile")],
                epilogue=lambda accs, biases, acc2, extras: extras[0] + (accs[0] + biases[0]),
                out_dtype=F32, name="attn_out")
            wgu = ffn_w_gate_up[k].astype(BF16)
            a = fused_mm(h, [(wgu, 0), (wgu, d_ff)], tm=tm, tn=512, n_out=d_ff, gain=norm_ffn[i],
                         epilogue=swiglu_up, out_dtype=BF16, name="ffn_up")
            h = fused_mm(a, [(ffn_w_down[k].astype(BF16), 0)], tm=512, tn=2048, n_out=d, tk=1792,
                         extras=[(h, "tile")], epilogue=resid_add, out_dtype=F32,
                         name="ffn_down")
        else:
            u = rms_norm(h, norm_mix[i], "s5_norm")
            weights = _s5_weights(s5_a_re[k], s5_a_im[k], s5_log_dt[k], s5_b_re[k], s5_b_im[k],
                                  s5_c_re[k], s5_c_im[k], levels)
            y = s5_mix(u, weights, s5_d[k], bsz, seq)
            wglu = s5_w_glu[k].astype(BF16)
            h = fused_mm(
                y, [(wglu, 0), (wglu, d)], tm=tm, tn=512, n_out=d,
                biases=[(s5_b_glu[k], 0), (s5_b_glu[k], d)], extras=[(h, "tile")],
                epilogue=lambda accs, biases, acc2, extras: extras[0] + (accs[0] + biases[0]) * jax.nn.sigmoid(accs[1] + biases[1]),
                out_dtype=F32, name="s5_glu")
            w_r = jnp.zeros((d, LANES), F32).at[:, :n_exp].set(moe_w_router[k].astype(F32))
            b_r = jnp.full((LANES,), -jnp.inf, F32).at[:n_exp].set(moe_b_router[k].astype(F32))
            route = fused_mm(h, [(w_r.astype(BF16), 0)], tm=tm, tn=LANES, n_out=LANES, gain=norm_ffn[i],
                             biases=[(b_r, 0)], epilogue=_router_epilogue, out_dtype=F32, name="router")
            row_token, row_gate, pos1, pos2, tile_expert, n_live = _route_plan(route, n_exp, MOE_TM)
            u = rms_norm(h, norm_ffn[i], "moe_norm")
            xs = gather_rows(u, row_token)
            wgu = moe_w_gate_up[k].astype(BF16)
            a = fused_mm(xs, [(wgu, 0), (wgu, d_ff)], tm=MOE_TM, tn=1024, n_out=d_ff,
                         group=(tile_expert, n_live), extras=[(row_gate, "rows")],
                         epilogue=gated_swiglu_up, out_dtype=BF16, name="moe_up")
            ys = fused_mm(a, [(moe_w_down[k].astype(BF16), 0)], tm=MOE_TM, tn=2048, n_out=d, tk=1792,
                          group=(tile_expert, n_live),
                          epilogue=lambda accs, biases, acc2, extras: accs[0], out_dtype=F32,
                          name="moe_down")
            h = combine_rows(h, ys, pos1, pos2)
        h = fused_mm(
            h, [(ple_w_gate[i].astype(BF16), 0)], tm=tm, tn=512, n_out=d, gain=norm_ple[i],
            pair2=(p[i].reshape(t, -1).astype(BF16), ple_w_proj[i].astype(BF16)),
            extras=[(h, "tile")], epilogue=ple_ep, out_dtype=F32, name="ple")
    out = rms_norm(h, norm_final, "final_norm")
    return out.reshape(bsz, seq, d).astype(x.dtype)
```

```python
import functools
import math

import jax
import jax.numpy as jnp
from jax import lax
from jax.experimental import pallas as pl
from jax.experimental.pallas import tpu as pltpu

F32 = jnp.float32
BF16 = jnp.bfloat16

HEAD_DIM = 64
KV_RATIO = 8
WINDOW = 128
ATTN_BLOCK = 128
ROT_DIM = HEAD_DIM // 4
ROPE_THETA = 500000.0
S5_GROUP = 16
S5_CHUNK = 8
RMS_EPS = 1e-6
LOG2E = math.log2(math.e)
LANES = 128
MOE_TM = 512
GATHER_ROWS = 256
VMEM_LIMIT = 60 * 1024 * 1024


def _cparams(sem):
    return pltpu.CompilerParams(dimension_semantics=sem, vmem_limit_bytes=VMEM_LIMIT)


def _fit(tile, n):
    if n <= tile:
        return n
    return max(c for c in range(LANES, tile + 1, LANES) if n % c == 0)


def _mm_body(*refs, norm, stage, grouped, n_rhs, n_bias, pair2, n_extra, nk, epilogue):
    it = iter(refs)
    nt_ref = None
    if grouped:
        next(it)
        nt_ref = next(it)
    x_ref = next(it)
    g_ref = next(it) if norm else None
    w_refs = [next(it) for _ in range(n_rhs)]
    b_refs = [next(it) for _ in range(n_bias)]
    x2_ref, w2_ref = (next(it), next(it)) if pair2 else (None, None)
    e_refs = [next(it) for _ in range(n_extra)]
    o_ref = next(it)
    xs_ref = next(it) if stage else None
    acc_refs = [next(it) for _ in range(n_rhs)] if nk > 1 else []

    i = pl.program_id(0)
    j = pl.program_id(1)
    k = pl.program_id(2)

    def compute():
        if stage:
            @pl.when(j == 0)
            def _():
                x = x_ref[...].astype(F32)
                if norm:
                    r = lax.rsqrt(jnp.mean(x * x, axis=-1, keepdims=True) + RMS_EPS)
                    x = (x * r) * g_ref[...]
                xs_ref[...] = x.astype(BF16)
            lhs = xs_ref[...]
        else:
            lhs = x_ref[...].astype(BF16)

        parts = [jnp.dot(lhs, w[...], preferred_element_type=F32) for w in w_refs]

        def finish(accs):
            biases = [b[...] for b in b_refs]
            acc2 = None
            if pair2:
                acc2 = jnp.dot(x2_ref[...].astype(BF16), w2_ref[...], preferred_element_type=F32)
            o_ref[...] = epilogue(accs, biases, acc2, [e[...] for e in e_refs]).astype(o_ref.dtype)

        if nk == 1:
            finish(parts)
        else:
            @pl.when(k == 0)
            def _():
                for a, p_ in zip(acc_refs, parts):
                    a[...] = p_

            @pl.when(k > 0)
            def _():
                for a, p_ in zip(acc_refs, parts):
                    a[...] += p_

            @pl.when(k == nk - 1)
            def _():
                finish([a[...] for a in acc_refs])

    if grouped:
        live = i < nt_ref[0]
        pl.when(live)(compute)

        @pl.when(jnp.logical_not(live) & (k == nk - 1))
        def _():
            o_ref[...] = jnp.zeros(o_ref.shape, o_ref.dtype)
    else:
        compute()


def fused_mm(x, ws, *, tm, tn, n_out, epilogue, out_dtype, gain=None, biases=(),
             pair2=None, extras=(), tk=None, group=None, name=None):
    m, kdim = x.shape
    tm = _fit(tm, m)
    tn = _fit(tn, n_out)
    tk = kdim if tk is None else _fit(tk, kdim)
    norm = gain is not None
    assert m % tm == 0 and n_out % tn == 0 and kdim % tk == 0
    nk = kdim // tk
    nj = n_out // tn
    assert not (norm and nk > 1)
    stage = norm or (nk == 1 and x.dtype != BF16)
    grouped = group is not None
    grid = (m // tm, nj, nk)

    if grouped:
        def ijk(i, j, k, pf):
            live = i < pf[1][0]
            return (jnp.minimum(i, pf[1][0] - 1), jnp.where(live, j, nj - 1), jnp.where(live, k, nk - 1))
    else:
        def ijk(i, j, k, pf):
            return i, j, k

    def x_map(i, j, k, *pf):
        i, j, k = ijk(i, j, k, pf)
        return (i, k)

    def w_map(ob, lead):
        def f(i, j, k, *pf):
            _, jj, kk = ijk(i, j, k, pf)
            expert = (pf[0][i],) if grouped else ()
            return lead + expert + (kk, jj + ob)
        return f

    def col_map(ob):
        def f(i, j, k, *pf):
            i, j, k = ijk(i, j, k, pf)
            return (0, j + ob)
        return f

    def row_map(i, j, k, *pf):
        i, j, k = ijk(i, j, k, pf)
        return (i, 0)

    def tile_map(i, j, k, *pf):
        i, j, k = ijk(i, j, k, pf)
        return (i, j)

    in_specs = [pl.BlockSpec((tm, tk), x_map)]
    args = [x]
    if norm:
        in_specs.append(pl.BlockSpec((1, kdim), lambda i, j, k, *pf: (0, 0)))
        args.append(gain.reshape(1, kdim).astype(F32))
    for w, off, lead in ws:
        assert off % tn == 0 and w.ndim == len(lead) + (3 if grouped else 2)
        wshape = (None,) * (w.ndim - 2) + (tk, tn)
        in_specs.append(pl.BlockSpec(wshape, w_map(off // tn, tuple(lead))))
        args.append(w)
    for b, off in biases:
        in_specs.append(pl.BlockSpec((1, tn), col_map(off // tn)))
        args.append(b.reshape(1, -1).astype(F32))
    if pair2 is not None:
        x2, w2 = pair2
        k2 = x2.shape[1]
        in_specs.append(pl.BlockSpec((tm, k2), row_map))
        in_specs.append(pl.BlockSpec((k2, tn), col_map(0)))
        args += [x2, w2]
    for arr, kind in extras:
        if kind == "tile":
            in_specs.append(pl.BlockSpec((tm, tn), tile_map))
        elif kind == "rows":
            in_specs.append(pl.BlockSpec((tm, arr.shape[1]), row_map))
        else:
            in_specs.append(pl.BlockSpec((1, tn), col_map(0)))
        args.append(arr)

    scratch = []
    if stage:
        scratch.append(pltpu.VMEM((tm, kdim), BF16))
    if nk > 1:
        scratch += [pltpu.VMEM((tm, tn), F32) for _ in ws]

    body = functools.partial(_mm_body, norm=norm, stage=stage, grouped=grouped, n_rhs=len(ws),
                             n_bias=len(biases), pair2=pair2 is not None, n_extra=len(extras),
                             nk=nk, epilogue=epilogue)
    grid_spec = pltpu.PrefetchScalarGridSpec(
        num_scalar_prefetch=2 if grouped else 0,
        grid=grid,
        in_specs=in_specs,
        out_specs=pl.BlockSpec((tm, tn), lambda i, j, k, *pf: (i, j)),
        scratch_shapes=scratch,
    )
    return pl.pallas_call(
        body,
        grid_spec=grid_spec,
        out_shape=jax.ShapeDtypeStruct((m, n_out), out_dtype),
        compiler_params=_cparams(("parallel", "arbitrary", "arbitrary")),
        name=name,
    )(*(tuple(group) if grouped else ()), *args)


def _rope_table_body(pos_ref, freq_ref, c_ref, s1_ref, s2_ref):
    ang = pos_ref[...].astype(F32) * freq_ref[...]
    lane = lax.broadcasted_iota(jnp.int32, ang.shape, 1) % HEAD_DIM
    cos = jnp.cos(ang)
    sin = jnp.sin(ang)
    half = ROT_DIM // 2
    c_ref[...] = jnp.where(lane < ROT_DIM, cos, 1.0)
    s1_ref[...] = jnp.where(lane < half, -sin, 0.0)
    s2_ref[...] = jnp.where((lane >= half) & (lane < ROT_DIM), sin, 0.0)


def rope_tables(positions):
    t = positions.size
    tm = min(1024, t)
    inv_freq = jnp.power(ROPE_THETA, -jnp.arange(0, ROT_DIM, 2, dtype=F32) / ROT_DIM)
    lane = jnp.arange(LANES) % HEAD_DIM
    freq = jnp.where(lane < ROT_DIM, inv_freq[lane % (ROT_DIM // 2)], 0.0).reshape(1, LANES)
    out = jax.ShapeDtypeStruct((t, LANES), F32)
    spec = pl.BlockSpec((tm, LANES), lambda i: (i, 0))
    return pl.pallas_call(
        _rope_table_body,
        grid=(t // tm,),
        in_specs=[pl.BlockSpec((tm, 1), lambda i: (i, 0)), pl.BlockSpec((1, LANES), lambda i: (0, 0))],
        out_specs=[spec, spec, spec],
        out_shape=[out, out, out],
        compiler_params=_cparams(("parallel",)),
        name="rope_tables",
    )(positions.reshape(t, 1), freq)


def _qkv_epilogue(accs, biases, acc2, extras):
    y = accs[0] + biases[0]
    c, s1, s2, m, sc = extras
    half = ROT_DIM // 2
    outs = []
    for ci in range(y.shape[1] // LANES):
        cols = slice(ci * LANES, (ci + 1) * LANES)
        xc = y[:, cols]
        rot = xc * c + pltpu.roll(xc, LANES - half, 1) * s1 + pltpu.roll(xc, half, 1) * s2
        outs.append((xc + m[:, cols] * (rot - xc)) * sc[:, cols])
    return jnp.concatenate(outs, axis=1)


def _attn_body(sink_ref, q_ref, kp_ref, kc_ref, vp_ref, vc_ref, o_ref, *, n_kv, group):
    n = pl.program_id(1)
    blk = ATTN_BLOCK
    cols = group * blk
    ki = lax.broadcasted_iota(jnp.int32, (2 * blk, cols), 0)
    qi = lax.broadcasted_iota(jnp.int32, (2 * blk, cols), 1) % blk
    rel = qi + blk - ki
    mask = (rel >= 0) & (rel < WINDOW) & ((n > 0) | (ki >= blk))
    vpt = vp_ref[...].astype(F32).T.astype(BF16)
    vct = vc_ref[...].astype(F32).T.astype(BF16)
    for h in range(n_kv):
        cs = slice(h * HEAD_DIM, (h + 1) * HEAD_DIM)
        kb = jnp.concatenate([kp_ref[:, cs], kc_ref[:, cs]], axis=0)
        vt = jnp.concatenate([vpt[cs, :], vct[cs, :]], axis=1)
        q = jnp.concatenate(
            [q_ref[:, (h * group + g) * HEAD_DIM:(h * group + g + 1) * HEAD_DIM] for g in range(group)],
            axis=0)
        sink = jnp.concatenate(
            [jnp.full((1, blk), sink_ref[h * group + g], F32) for g in range(group)], axis=1)
        st = lax.dot_general(kb, q, (((1,), (1,)), ((), ())), preferred_element_type=F32)
        st = jnp.where(mask, st, -jnp.inf)
        mx = jnp.maximum(jnp.max(st, axis=0, keepdims=True), sink)
        e = jnp.exp2(st - mx)
        den = jnp.sum(e, axis=0, keepdims=True) + jnp.exp2(sink - mx)
        ot = jnp.dot(vt, e.astype(BF16), preferred_element_type=F32) / den
        for g2 in range(group // 2):
            pair = jnp.concatenate([ot[:, (2 * g2) * blk:(2 * g2 + 1) * blk],
                                    ot[:, (2 * g2 + 1) * blk:(2 * g2 + 2) * blk]], axis=0)
            c0 = (h * group + 2 * g2) * HEAD_DIM
            o_ref[:, c0:c0 + 2 * HEAD_DIM] = pair.T.astype(o_ref.dtype)


def attention(qkv, sinks, bsz, seq):
    t, width = qkv.shape
    n_q = sinks.shape[0]
    n_kv = n_q // KV_RATIO if n_q >= KV_RATIO else 1
    group = n_q // n_kv
    dq = n_q * HEAD_DIM
    dkv = n_kv * HEAD_DIM
    assert width == dq + 2 * dkv and dq % dkv == 0 and group % 2 == 0
    nb = seq // ATTN_BLOCK
    kblk = dq // dkv
    cur = lambda b, n, s: (b * nb + n)
    prev = lambda b, n, s: (b * nb + jnp.maximum(n - 1, 0))
    grid_spec = pltpu.PrefetchScalarGridSpec(
        num_scalar_prefetch=1,
        grid=(bsz, nb),
        in_specs=[
            pl.BlockSpec((ATTN_BLOCK, dq), lambda b, n, s: (cur(b, n, s), 0)),
            pl.BlockSpec((ATTN_BLOCK, dkv), lambda b, n, s: (prev(b, n, s), kblk)),
            pl.BlockSpec((ATTN_BLOCK, dkv), lambda b, n, s: (cur(b, n, s), kblk)),
            pl.BlockSpec((ATTN_BLOCK, dkv), lambda b, n, s: (prev(b, n, s), kblk + 1)),
            pl.BlockSpec((ATTN_BLOCK, dkv), lambda b, n, s: (cur(b, n, s), kblk + 1)),
        ],
        out_specs=pl.BlockSpec((ATTN_BLOCK, dq), lambda b, n, s: (cur(b, n, s), 0)),
    )
    return pl.pallas_call(
        functools.partial(_attn_body, n_kv=n_kv, group=group),
        grid_spec=grid_spec,
        out_shape=jax.ShapeDtypeStruct((t, dq), BF16),
        compiler_params=_cparams(("parallel", "arbitrary")),
        name="swa_attention",
    )(sinks.astype(F32) * LOG2E, qkv, qkv, qkv, qkv, qkv)


def _norm_body(x_ref, g_ref, o_ref):
    x = x_ref[...].astype(F32)
    r = lax.rsqrt(jnp.mean(x * x, axis=-1, keepdims=True) + RMS_EPS)
    o_ref[...] = ((x * r) * g_ref[...]).astype(o_ref.dtype)


def rms_norm(x, g, name):
    t, d = x.shape
    tm = min(1024, t)
    return pl.pallas_call(
        _norm_body,
        grid=(t // tm,),
        in_specs=[pl.BlockSpec((tm, d), lambda i: (i, 0)), pl.BlockSpec((1, d), lambda i: (0, 0))],
        out_specs=pl.BlockSpec((tm, d), lambda i: (i, 0)),
        out_shape=jax.ShapeDtypeStruct((t, d), F32),
        compiler_params=_cparams(("parallel",)),
        name=name,
    )(x, g.reshape(1, d).astype(F32))


def _s5_body(u_ref, win_ref, tp_ref, wout_ref, pwr_ref, pwi_ref, d_ref, o_ref, u2_ref, *, levels):
    L = S5_CHUNK
    rows = u_ref.shape[0] // L
    for s in range(L):
        u2_ref[:, s * LANES:(s + 1) * LANES] = u_ref[pl.ds(s, rows, stride=L), :].astype(BF16)
    u2 = u2_ref[...]
    xin = jnp.dot(u2, win_ref[0], preferred_element_type=F32)
    hw = xin.shape[1] // 2
    hr, hi = xin[:, :hw], xin[:, hw:]
    row = lax.broadcasted_iota(jnp.int32, (rows, hw), 0)
    for lv in range(levels):
        d = 1 << lv
        ar = pwr_ref[0, lv:lv + 1, :]
        ai = pwi_ref[0, lv:lv + 1, :]
        keep = row >= d
        sr = jnp.where(keep, pltpu.roll(hr, d, 0), 0.0)
        si = jnp.where(keep, pltpu.roll(hi, d, 0), 0.0)
        hr, hi = hr + (ar * sr - ai * si), hi + (ar * si + ai * sr)
    keep = row >= 1
    pr = jnp.where(keep, pltpu.roll(hr, 1, 0), 0.0)
    pi_ = jnp.where(keep, pltpu.roll(hi, 1, 0), 0.0)
    hprev = jnp.concatenate([pr, pi_], axis=1).astype(BF16)
    y = (jnp.dot(u2, tp_ref[0], preferred_element_type=F32)
         + jnp.dot(hprev, wout_ref[0], preferred_element_type=F32))
    dsk = d_ref[...]
    for t in range(L):
        yt = y[:, t * LANES:(t + 1) * LANES] + dsk * u_ref[pl.ds(t, rows, stride=L), :]
        o_ref[pl.ds(t, rows, stride=L), :] = 0.5 * yt * (1.0 + lax.erf(yt * (2.0 ** -0.5)))


def _s5_weights(a_re, a_im, log_dt, b_re, b_im, c_re, c_im, levels):
    hp = lax.Precision.HIGHEST
    L = S5_CHUNK
    g, n = a_re.shape
    nc = g * S5_GROUP // LANES
    ga = LANES // S5_GROUP
    dt = jnp.exp(log_dt.astype(F32))[:, None]
    ar = a_re.astype(F32)
    ai = a_im.astype(F32)
    mag = jnp.exp(ar * dt)
    lr = mag * jnp.cos(ai * dt)
    li = mag * jnp.sin(ai * dt)
    den = ar * ar + ai * ai
    zr = ((lr - 1.0) * ar + li * ai) / den
    zi = (li * ar - (lr - 1.0) * ai) / den
    br = b_re.astype(F32)
    bi = b_im.astype(F32)
    bbr = zr[..., None] * br - zi[..., None] * bi
    bbi = zr[..., None] * bi + zi[..., None] * br
    cr = c_re.astype(F32)
    ci = c_im.astype(F32)

    def lam_pow(e):
        e = e.astype(F32)[:, None, None]
        m_ = jnp.exp(e * (ar * dt))
        return m_ * jnp.cos(e * (ai * dt)), m_ * jnp.sin(e * (ai * dt))

    pr, pi_ = lam_pow(jnp.arange(L + 1))
    gr = cr[None] * pr[:L, :, None, :] - ci[None] * pi_[:L, :, None, :]
    gi = cr[None] * pi_[:L, :, None, :] + ci[None] * pr[:L, :, None, :]
    kmat = (jnp.einsum("tgon,gni->tgoi", gr, bbr, precision=hp)
            - jnp.einsum("tgon,gni->tgoi", gi, bbi, precision=hp))
    eye = jnp.eye(ga, dtype=F32)
    kx = jnp.einsum("tcaoi,ab->ctaibo", kmat.reshape(L, nc, ga, S5_GROUP, S5_GROUP), eye)
    kx = kx.reshape(nc, L, LANES, LANES).astype(BF16)
    zero = jnp.zeros((nc, LANES, LANES), BF16)
    tp_c = jnp.concatenate(
        [jnp.concatenate([kx[:, t_ - s_] if t_ >= s_ else zero for t_ in range(L)], axis=2)
         for s_ in range(L)], axis=1)
    qr, qi = pr[L - 1 - jnp.arange(L)], pi_[L - 1 - jnp.arange(L)]
    wr = qr[..., None] * bbr[None] - qi[..., None] * bbi[None]
    wi = qr[..., None] * bbi[None] + qi[..., None] * bbr[None]
    win = jnp.stack([wr, wi]).reshape(2, L, nc, ga, n, S5_GROUP)
    win_c = jnp.einsum("rscani,ab->csairbn", win, eye).reshape(nc, L * LANES, 2 * ga * n)
    er, ei = pr[1:], pi_[1:]
    g_r = cr[None] * er[:, :, None, :] - ci[None] * ei[:, :, None, :]
    g_i = cr[None] * ei[:, :, None, :] + ci[None] * er[:, :, None, :]
    wo = jnp.stack([g_r, -g_i]).reshape(2, L, nc, ga, S5_GROUP, n)
    wout_c = jnp.einsum("rtcaon,ab->crantbo", wo, eye).reshape(nc, 2 * ga * n, L * LANES)
    sr, si = lam_pow(L * (2 ** jnp.arange(levels)))
    pwr = sr.reshape(levels, nc, ga * n).transpose(1, 0, 2)
    pwi = si.reshape(levels, nc, ga * n).transpose(1, 0, 2)
    return win_c.astype(BF16), tp_c, wout_c.astype(BF16), pwr, pwi


def s5_mix(u, weights, d_skip, bsz, seq):
    t, d = u.shape
    win_c, tp_c, wout_c, pwr, pwi = weights
    nc = d // LANES
    levels = pwr.shape[1]
    L = S5_CHUNK
    rows = seq // L
    wspec = lambda a: pl.BlockSpec((1,) + a.shape[1:], lambda c, b: (c, 0, 0))
    return pl.pallas_call(
        functools.partial(_s5_body, levels=levels),
        grid=(nc, bsz),
        in_specs=[
            pl.BlockSpec((seq, LANES), lambda c, b: (b, c)),
            wspec(win_c), wspec(tp_c), wspec(wout_c), wspec(pwr), wspec(pwi),
            pl.BlockSpec((1, LANES), lambda c, b: (0, c)),
        ],
        out_specs=pl.BlockSpec((seq, LANES), lambda c, b: (b, c)),
        out_shape=jax.ShapeDtypeStruct((t, d), F32),
        scratch_shapes=[pltpu.VMEM((rows, L * LANES), BF16)],
        compiler_params=_cparams(("parallel", "arbitrary")),
        name="s5_chunked",
    )(u, win_c, tp_c, wout_c, pwr, pwi, d_skip.reshape(1, d).astype(F32))


def _router_epilogue(accs, biases, acc2, extras):
    logits = accs[0] + biases[0]
    lane = lax.broadcasted_iota(jnp.int32, logits.shape, 1)
    big = jnp.int32(logits.shape[1])
    m1 = jnp.max(logits, axis=-1, keepdims=True)
    i1 = jnp.min(jnp.where(logits == m1, lane, big), axis=-1, keepdims=True)
    rest = jnp.where(lane == i1, -jnp.inf, logits)
    m2 = jnp.max(rest, axis=-1, keepdims=True)
    i2 = jnp.min(jnp.where(rest == m2, lane, big), axis=-1, keepdims=True)
    e2 = jnp.exp(m2 - m1)
    den = 1.0 + e2
    rec = jnp.where(lane == 0, 1.0 / den, 0.0) + jnp.where(lane == 1, e2 / den, 0.0)
    rec = rec + jnp.where(lane == 2, i1.astype(F32), 0.0) + jnp.where(lane == 3, i2.astype(F32), 0.0)
    return rec


def _route_plan(route, n_exp, tm):
    t = route.shape[0]
    e_flat = route[:, 2:4].astype(jnp.int32).T.reshape(-1)
    onehot = (e_flat[:, None] == jnp.arange(n_exp, dtype=jnp.int32)[None, :]).astype(jnp.int32)
    csum = jnp.cumsum(onehot, axis=0)
    rank = jnp.sum(csum * onehot, axis=1) - 1
    counts = csum[-1]
    padded = ((counts + tm - 1) // tm) * tm
    ends = jnp.cumsum(padded)
    starts = ends - padded
    pos = starts[e_flat] + rank
    n_tiles = (2 * t + n_exp * (tm - 1)) // tm
    n_live = (ends[-1] // tm).astype(jnp.int32)
    tile_start = jnp.arange(n_tiles, dtype=jnp.int32) * tm
    tile_expert = jnp.sum((ends[None, :] <= tile_start[:, None]).astype(jnp.int32), axis=1)
    tile_expert = jnp.minimum(tile_expert, tile_expert[n_live - 1]).astype(jnp.int32)
    rows = n_tiles * tm
    tok = jnp.tile(jnp.arange(t, dtype=jnp.int32), 2)
    row_token = jnp.zeros((rows,), jnp.int32).at[pos].set(tok, unique_indices=True)
    return row_token, pos[:t], pos[t:], tile_expert, n_live.reshape(1)


def _gather_body(idx_ref, src_ref, o_ref, sem):
    n = o_ref.shape[0]

    def copy(r):
        return pltpu.make_async_copy(src_ref.at[pl.ds(idx_ref[0, 0, r], 1)], o_ref.at[pl.ds(r, 1)], sem)

    def start(r, c):
        copy(r).start()
        return c

    def wait(r, c):
        copy(r).wait()
        return c

    lax.fori_loop(0, n, start, 0, unroll=8)
    lax.fori_loop(0, n, wait, 0, unroll=8)


def gather_rows(src, row_idx):
    rows = row_idx.shape[0]
    d = src.shape[1]
    tg = _fit(GATHER_ROWS, rows)
    return pl.pallas_call(
        _gather_body,
        grid=(rows // tg,),
        in_specs=[
            pl.BlockSpec((1, 1, tg), lambda i: (i, 0, 0), memory_space=pltpu.SMEM),
            pl.BlockSpec(memory_space=pl.ANY),
        ],
        out_specs=pl.BlockSpec((tg, d), lambda i: (i, 0)),
        out_shape=jax.ShapeDtypeStruct((rows, d), src.dtype),
        scratch_shapes=[pltpu.SemaphoreType.DMA(())],
        compiler_params=_cparams(("arbitrary",)),
        name="moe_gather",
    )(row_idx.reshape(rows // tg, 1, tg), src)


def _combine_body(i1_ref, i2_ref, ys_ref, route_ref, h_ref, o_ref, buf_ref, sem):
    n = o_ref.shape[0]

    def copies(r):
        return (pltpu.make_async_copy(ys_ref.at[pl.ds(i1_ref[0, 0, r], 1)], buf_ref.at[0, pl.ds(r, 1)], sem.at[0]),
                pltpu.make_async_copy(ys_ref.at[pl.ds(i2_ref[0, 0, r], 1)], buf_ref.at[1, pl.ds(r, 1)], sem.at[1]))

    def start(r, c):
        for cp in copies(r):
            cp.start()
        return c

    def wait(r, c):
        for cp in copies(r):
            cp.wait()
        return c

    lax.fori_loop(0, n, start, 0, unroll=8)
    lax.fori_loop(0, n, wait, 0, unroll=8)
    gates = route_ref[...]
    o_ref[...] = h_ref[...] + (gates[:, 0:1] * buf_ref[0] + gates[:, 1:2] * buf_ref[1])


def combine_rows(h, ys, route, pos1, pos2):
    t, d = h.shape
    tg = _fit(GATHER_ROWS, t)
    ispec = pl.BlockSpec((1, 1, tg), lambda i: (i, 0, 0), memory_space=pltpu.SMEM)
    return pl.pallas_call(
        _combine_body,
        grid=(t // tg,),
        in_specs=[ispec, ispec, pl.BlockSpec(memory_space=pl.ANY),
                  pl.BlockSpec((tg, route.shape[1]), lambda i: (i, 0)),
                  pl.BlockSpec((tg, d), lambda i: (i, 0))],
        out_specs=pl.BlockSpec((tg, d), lambda i: (i, 0)),
        out_shape=jax.ShapeDtypeStruct((t, d), h.dtype),
        scratch_shapes=[pltpu.VMEM((2, tg, d), ys.dtype), pltpu.SemaphoreType.DMA((2,))],
        compiler_params=_cparams(("arbitrary",)),
        name="moe_combine",
    )(pos1.reshape(t // tg, 1, tg), pos2.reshape(t // tg, 1, tg), ys, route, h)


def kernel(x, p, positions, norm_mix, norm_ffn, norm_ple, norm_final, attn_w_qkv, attn_b_qkv, attn_w_o, attn_b_o, attn_sinks, s5_a_re, s5_a_im, s5_log_dt, s5_b_re, s5_b_im, s5_c_re, s5_c_im, s5_d, s5_w_glu, s5_b_glu, ffn_w_gate_up, ffn_w_down, moe_w_router, moe_b_router, moe_w_gate_up, moe_w_down, ple_w_proj, ple_w_gate):
    bsz, seq, d = x.shape
    depth = p.shape[0]
    t = bsz * seq
    d_ff = ffn_w_down.shape[1]
    n_exp = moe_w_router.shape[-1]
    qkv_w = attn_w_qkv.shape[-1]
    dq = attn_w_o.shape[1]
    tm = 1024

    h = x.reshape(t, d).astype(F32)
    rope_c, rope_s1, rope_s2 = rope_tables(positions)
    col = jnp.arange(qkv_w)
    rope_cols = (col < dq + (qkv_w - dq) // 2).astype(F32).reshape(1, qkv_w)
    q_scale = jnp.where(col < dq, (HEAD_DIM ** -0.5) * LOG2E, 1.0).astype(F32).reshape(1, qkv_w)
    levels = max(1, int(math.ceil(math.log2(seq // S5_CHUNK))))
    w_qkv, w_o = attn_w_qkv.astype(BF16), attn_w_o.astype(BF16)
    w_ffn_up, w_ffn_down = ffn_w_gate_up.astype(BF16), ffn_w_down.astype(BF16)
    w_glu = s5_w_glu.astype(BF16)
    w_moe_up, w_moe_down = moe_w_gate_up.astype(BF16), moe_w_down.astype(BF16)
    w_ple_gate, w_ple_proj = ple_w_gate.astype(BF16), ple_w_proj.astype(BF16)
    p_rows = p.reshape(depth, t, -1).astype(BF16)

    def resid_add(accs, biases, acc2, extras):
        return extras[0] + accs[0]

    def swiglu_up(accs, biases, acc2, extras):
        return jax.nn.silu(accs[0]) * accs[1]

    def ple_ep(accs, biases, acc2, extras):
        return extras[0] + jax.nn.sigmoid(accs[0]) * acc2

    for i in range(depth):
        k = i // 2
        if i % 2 == 0:
            qkv = fused_mm(
                h, [(w_qkv, 0, (k,))], tm=tm, tn=512, n_out=qkv_w,
                gain=norm_mix[i], biases=[(attn_b_qkv[k], 0)],
                extras=[(rope_c, "rows"), (rope_s1, "rows"), (rope_s2, "rows"), (rope_cols, "cols"),
                        (q_scale, "cols")],
                epilogue=_qkv_epilogue, out_dtype=BF16, name="qkv_rope")
            o = attention(qkv, attn_sinks[k], bsz, seq)
            h = fused_mm(
                o, [(w_o, 0, (k,))], tm=tm, tn=1024, n_out=d,
                biases=[(attn_b_o[k], 0)], extras=[(h, "tile")],
                epilogue=lambda accs, biases, acc2, extras: extras[0] + (accs[0] + biases[0]),
                out_dtype=F32, name="attn_out")
            a = fused_mm(h, [(w_ffn_up, 0, (k,)), (w_ffn_up, d_ff, (k,))], tm=tm, tn=512, n_out=d_ff,
                         gain=norm_ffn[i], epilogue=swiglu_up, out_dtype=BF16, name="ffn_up")
            h = fused_mm(a, [(w_ffn_down, 0, (k,))], tm=512, tn=2048, n_out=d, tk=1792,
                         extras=[(h, "tile")], epilogue=resid_add, out_dtype=F32,
                         name="ffn_down")
        else:
            u = rms_norm(h, norm_mix[i], "s5_norm")
            weights = _s5_weights(s5_a_re[k], s5_a_im[k], s5_log_dt[k], s5_b_re[k], s5_b_im[k],
                                  s5_c_re[k], s5_c_im[k], levels)
            y = s5_mix(u, weights, s5_d[k], bsz, seq)
            h = fused_mm(
                y, [(w_glu, 0, (k,)), (w_glu, d, (k,))], tm=tm, tn=512, n_out=d,
                biases=[(s5_b_glu[k], 0), (s5_b_glu[k], d)], extras=[(h, "tile")],
                epilogue=lambda accs, biases, acc2, extras: extras[0] + (accs[0] + biases[0]) * jax.nn.sigmoid(accs[1] + biases[1]),
                out_dtype=F32, name="s5_glu")
            w_r = jnp.zeros((d, LANES), F32).at[:, :n_exp].set(moe_w_router[k].astype(F32))
            b_r = jnp.full((LANES,), -jnp.inf, F32).at[:n_exp].set(moe_b_router[k].astype(F32))
            route = fused_mm(h, [(w_r.astype(BF16), 0, ())], tm=tm, tn=LANES, n_out=LANES, gain=norm_ffn[i],
                             biases=[(b_r, 0)], epilogue=_router_epilogue, out_dtype=F32, name="router")
            row_token, pos1, pos2, tile_expert, n_live = _route_plan(route, n_exp, MOE_TM)
            xs = gather_rows(h, row_token)
            a = fused_mm(xs, [(w_moe_up, 0, (k,)), (w_moe_up, d_ff, (k,))], tm=MOE_TM, tn=1024, n_out=d_ff,
                         gain=norm_ffn[i], group=(tile_expert, n_live),
                         epilogue=swiglu_up, out_dtype=BF16, name="moe_up")
            ys = fused_mm(a, [(w_moe_down, 0, (k,))], tm=MOE_TM, tn=2048, n_out=d, tk=1792,
                          group=(tile_expert, n_live),
                          epilogue=lambda accs, biases, acc2, extras: accs[0], out_dtype=F32,
                          name="moe_down")
            h = combine_rows(h, ys, route, pos1, pos2)
        h = fused_mm(
            h, [(w_ple_gate, 0, (i,))], tm=tm, tn=512, n_out=d, gain=norm_ple[i],
            pair2=(p_rows[i], w_ple_proj[i]),
            extras=[(h, "tile")], epilogue=ple_ep, out_dtype=F32, name="ple")
    out = rms_norm(h, norm_final, "final_norm")
    return out.reshape(bsz, seq, d).astype(x.dtype)
```

```python
import functools
import math

import jax
import jax.numpy as jnp
from jax import lax
from jax.experimental import pallas as pl
from jax.experimental.pallas import tpu as pltpu

F32 = jnp.float32
BF16 = jnp.bfloat16

HEAD_DIM = 64
KV_RATIO = 8
WINDOW = 128
ATTN_BLOCK = 128
ROT_DIM = HEAD_DIM // 4
ROPE_THETA = 500000.0
S5_GROUP = 16
S5_CHUNK = 8
RMS_EPS = 1e-6
LOG2E = math.log2(math.e)
LANES = 128
MOE_TM = 512
GATHER_ROWS = 256
VMEM_LIMIT = 60 * 1024 * 1024


def _cparams(sem):
    return pltpu.CompilerParams(dimension_semantics=sem, vmem_limit_bytes=VMEM_LIMIT)


def _fit(tile, n):
    if n <= tile:
        return n
    return max(c for c in range(LANES, tile + 1, LANES) if n % c == 0)


def _mm_body(*refs, norm, stage, grouped, n_rhs, n_bias, pair2, n_extra, nk, x_cols, epilogue):
    it = iter(refs)
    nt_ref = None
    if grouped:
        next(it)
        nt_ref = next(it)
    x_ref = next(it)
    g_ref = next(it) if norm else None
    w_refs = [next(it) for _ in range(n_rhs)]
    b_refs = [next(it) for _ in range(n_bias)]
    x2_ref, w2_ref = (next(it), next(it)) if pair2 else (None, None)
    e_refs = [next(it) for _ in range(n_extra)]
    o_ref = next(it)
    xs_ref = next(it) if stage else None
    acc_refs = [next(it) for _ in range(n_rhs)] if nk > 1 else []

    i = pl.program_id(0)
    j = pl.program_id(1)
    k = pl.program_id(2)

    def compute():
        if stage:
            @pl.when(j == 0)
            def _():
                x = x_ref[...].astype(F32)
                if norm:
                    r = lax.rsqrt(jnp.mean(x * x, axis=-1, keepdims=True) + RMS_EPS)
                    x = (x * r) * g_ref[...]
                xs_ref[...] = x.astype(BF16)
            lhs = xs_ref[...]
        else:
            lhs = x_ref[...].astype(BF16)

        parts = [jnp.dot(lhs, w[...], preferred_element_type=F32) for w in w_refs]

        def finish(accs):
            biases = [b[...] for b in b_refs]
            acc2 = None
            if pair2:
                acc2 = jnp.dot(x2_ref[...].astype(BF16), w2_ref[...], preferred_element_type=F32)
            extras = [e[...] for e in e_refs]
            if x_cols:
                tn = o_ref.shape[1]
                extras.append(x_ref[:, pl.ds(pl.multiple_of(j * tn, LANES), tn)].astype(F32))
            o_ref[...] = epilogue(accs, biases, acc2, extras).astype(o_ref.dtype)

        if nk == 1:
            finish(parts)
        else:
            @pl.when(k == 0)
            def _():
                for a, p_ in zip(acc_refs, parts):
                    a[...] = p_

            @pl.when(k > 0)
            def _():
                for a, p_ in zip(acc_refs, parts):
                    a[...] += p_

            @pl.when(k == nk - 1)
            def _():
                finish([a[...] for a in acc_refs])

    if grouped:
        live = i < nt_ref[0]
        pl.when(live)(compute)

        @pl.when(jnp.logical_not(live) & (k == nk - 1))
        def _():
            o_ref[...] = jnp.zeros(o_ref.shape, o_ref.dtype)
    else:
        compute()


def fused_mm(x, ws, *, tm, tn, n_out, epilogue, out_dtype, gain=None, biases=(),
             pair2=None, extras=(), x_cols=False, tk=None, group=None, name=None):
    m, kdim = x.shape
    tm = _fit(tm, m)
    tn = _fit(tn, n_out)
    tk = kdim if tk is None else _fit(tk, kdim)
    norm = gain is not None
    assert m % tm == 0 and n_out % tn == 0 and kdim % tk == 0
    nk = kdim // tk
    nj = n_out // tn
    assert not (norm and nk > 1)
    assert not x_cols or (nk == 1 and kdim == n_out)
    stage = norm or (nk == 1 and x.dtype != BF16)
    grouped = group is not None
    grid = (m // tm, nj, nk)

    if grouped:
        def ijk(i, j, k, pf):
            live = i < pf[1][0]
            return (jnp.minimum(i, pf[1][0] - 1), jnp.where(live, j, nj - 1), jnp.where(live, k, nk - 1))
    else:
        def ijk(i, j, k, pf):
            return i, j, k

    def x_map(i, j, k, *pf):
        i, j, k = ijk(i, j, k, pf)
        return (i, k)

    def w_map(ob, lead):
        def f(i, j, k, *pf):
            _, jj, kk = ijk(i, j, k, pf)
            expert = (pf[0][i],) if grouped else ()
            return lead + expert + (kk, jj + ob)
        return f

    def col_map(ob):
        def f(i, j, k, *pf):
            i, j, k = ijk(i, j, k, pf)
            return (0, j + ob)
        return f

    def row_map(i, j, k, *pf):
        i, j, k = ijk(i, j, k, pf)
        return (i, 0)

    def tile_map(i, j, k, *pf):
        i, j, k = ijk(i, j, k, pf)
        return (i, j)

    in_specs = [pl.BlockSpec((tm, tk), x_map)]
    args = [x]
    if norm:
        in_specs.append(pl.BlockSpec((1, kdim), lambda i, j, k, *pf: (0, 0)))
        args.append(gain.reshape(1, kdim).astype(F32))
    for w, off, lead in ws:
        assert off % tn == 0 and w.ndim == len(lead) + (3 if grouped else 2)
        wshape = (None,) * (w.ndim - 2) + (tk, tn)
        in_specs.append(pl.BlockSpec(wshape, w_map(off // tn, tuple(lead))))
        args.append(w)
    for b, off in biases:
        in_specs.append(pl.BlockSpec((1, tn), col_map(off // tn)))
        args.append(b.reshape(1, -1).astype(F32))
    if pair2 is not None:
        x2, w2 = pair2
        k2 = x2.shape[1]
        in_specs.append(pl.BlockSpec((tm, k2), row_map))
        in_specs.append(pl.BlockSpec((k2, tn), col_map(0)))
        args += [x2, w2]
    for arr, kind in extras:
        if kind == "tile":
            in_specs.append(pl.BlockSpec((tm, tn), tile_map))
        elif kind == "rows":
            in_specs.append(pl.BlockSpec((tm, arr.shape[1]), row_map))
        else:
            in_specs.append(pl.BlockSpec((1, tn), col_map(0)))
        args.append(arr)

    scratch = []
    if stage:
        scratch.append(pltpu.VMEM((tm, kdim), BF16))
    if nk > 1:
        scratch += [pltpu.VMEM((tm, tn), F32) for _ in ws]

    body = functools.partial(_mm_body, norm=norm, stage=stage, grouped=grouped, n_rhs=len(ws),
                             n_bias=len(biases), pair2=pair2 is not None, n_extra=len(extras),
                             nk=nk, x_cols=x_cols, epilogue=epilogue)
    grid_spec = pltpu.PrefetchScalarGridSpec(
        num_scalar_prefetch=2 if grouped else 0,
        grid=grid,
        in_specs=in_specs,
        out_specs=pl.BlockSpec((tm, tn), lambda i, j, k, *pf: (i, j)),
        scratch_shapes=scratch,
    )
    return pl.pallas_call(
        body,
        grid_spec=grid_spec,
        out_shape=jax.ShapeDtypeStruct((m, n_out), out_dtype),
        compiler_params=_cparams(("parallel", "arbitrary", "arbitrary")),
        name=name,
    )(*(tuple(group) if grouped else ()), *args)


def _rope_table_body(pos_ref, freq_ref, c_ref, s1_ref, s2_ref):
    ang = pos_ref[...].astype(F32) * freq_ref[...]
    lane = lax.broadcasted_iota(jnp.int32, ang.shape, 1) % HEAD_DIM
    cos = jnp.cos(ang)
    sin = jnp.sin(ang)
    half = ROT_DIM // 2
    c_ref[...] = jnp.where(lane < ROT_DIM, cos, 1.0)
    s1_ref[...] = jnp.where(lane < half, -sin, 0.0)
    s2_ref[...] = jnp.where((lane >= half) & (lane < ROT_DIM), sin, 0.0)


def rope_tables(positions):
    t = positions.size
    tm = min(1024, t)
    inv_freq = jnp.power(ROPE_THETA, -jnp.arange(0, ROT_DIM, 2, dtype=F32) / ROT_DIM)
    lane = jnp.arange(LANES) % HEAD_DIM
    freq = jnp.where(lane < ROT_DIM, inv_freq[lane % (ROT_DIM // 2)], 0.0).reshape(1, LANES)
    out = jax.ShapeDtypeStruct((t, LANES), F32)
    spec = pl.BlockSpec((tm, LANES), lambda i: (i, 0))
    return pl.pallas_call(
        _rope_table_body,
        grid=(t // tm,),
        in_specs=[pl.BlockSpec((tm, 1), lambda i: (i, 0)), pl.BlockSpec((1, LANES), lambda i: (0, 0))],
        out_specs=[spec, spec, spec],
        out_shape=[out, out, out],
        compiler_params=_cparams(("parallel",)),
        name="rope_tables",
    )(positions.reshape(t, 1), freq)


def _qkv_epilogue(accs, biases, acc2, extras):
    y = accs[0] + biases[0]
    c, s1, s2, m, sc = extras
    half = ROT_DIM // 2
    outs = []
    for ci in range(y.shape[1] // LANES):
        cols = slice(ci * LANES, (ci + 1) * LANES)
        xc = y[:, cols]
        rot = xc * c + pltpu.roll(xc, LANES - half, 1) * s1 + pltpu.roll(xc, half, 1) * s2
        outs.append((xc + m[:, cols] * (rot - xc)) * sc[:, cols])
    return jnp.concatenate(outs, axis=1)


def _attn_body(sink_ref, q_ref, kp_ref, kc_ref, vp_ref, vc_ref, o_ref, *, n_kv, group):
    n = pl.program_id(1)
    blk = ATTN_BLOCK
    cols = group * blk
    ki = lax.broadcasted_iota(jnp.int32, (2 * blk, cols), 0)
    qi = lax.broadcasted_iota(jnp.int32, (2 * blk, cols), 1) % blk
    rel = qi + blk - ki
    mask = (rel >= 0) & (rel < WINDOW) & ((n > 0) | (ki >= blk))
    vpt = vp_ref[...].astype(F32).T.astype(BF16)
    vct = vc_ref[...].astype(F32).T.astype(BF16)
    for h in range(n_kv):
        cs = slice(h * HEAD_DIM, (h + 1) * HEAD_DIM)
        kb = jnp.concatenate([kp_ref[:, cs], kc_ref[:, cs]], axis=0)
        vt = jnp.concatenate([vpt[cs, :], vct[cs, :]], axis=1)
        q = jnp.concatenate(
            [q_ref[:, (h * group + g) * HEAD_DIM:(h * group + g + 1) * HEAD_DIM] for g in range(group)],
            axis=0)
        sink = jnp.concatenate(
            [jnp.full((1, blk), sink_ref[h * group + g], F32) for g in range(group)], axis=1)
        st = lax.dot_general(kb, q, (((1,), (1,)), ((), ())), preferred_element_type=F32)
        st = jnp.where(mask, st, -jnp.inf)
        mx = jnp.maximum(jnp.max(st, axis=0, keepdims=True), sink)
        e = jnp.exp2(st - mx)
        den = jnp.sum(e, axis=0, keepdims=True) + jnp.exp2(sink - mx)
        ot = jnp.dot(vt, e.astype(BF16), preferred_element_type=F32) / den
        for g2 in range(group // 2):
            pair = jnp.concatenate([ot[:, (2 * g2) * blk:(2 * g2 + 1) * blk],
                                    ot[:, (2 * g2 + 1) * blk:(2 * g2 + 2) * blk]], axis=0)
            c0 = (h * group + 2 * g2) * HEAD_DIM
            o_ref[:, c0:c0 + 2 * HEAD_DIM] = pair.T.astype(o_ref.dtype)


def attention(qkv, sinks, bsz, seq):
    t, width = qkv.shape
    n_q = sinks.shape[0]
    n_kv = n_q // KV_RATIO if n_q >= KV_RATIO else 1
    group = n_q // n_kv
    dq = n_q * HEAD_DIM
    dkv = n_kv * HEAD_DIM
    assert width == dq + 2 * dkv and dq % dkv == 0 and group % 2 == 0
    nb = seq // ATTN_BLOCK
    kblk = dq // dkv
    cur = lambda b, n, s: (b * nb + n)
    prev = lambda b, n, s: (b * nb + jnp.maximum(n - 1, 0))
    grid_spec = pltpu.PrefetchScalarGridSpec(
        num_scalar_prefetch=1,
        grid=(bsz, nb),
        in_specs=[
            pl.BlockSpec((ATTN_BLOCK, dq), lambda b, n, s: (cur(b, n, s), 0)),
            pl.BlockSpec((ATTN_BLOCK, dkv), lambda b, n, s: (prev(b, n, s), kblk)),
            pl.BlockSpec((ATTN_BLOCK, dkv), lambda b, n, s: (cur(b, n, s), kblk)),
            pl.BlockSpec((ATTN_BLOCK, dkv), lambda b, n, s: (prev(b, n, s), kblk + 1)),
            pl.BlockSpec((ATTN_BLOCK, dkv), lambda b, n, s: (cur(b, n, s), kblk + 1)),
        ],
        out_specs=pl.BlockSpec((ATTN_BLOCK, dq), lambda b, n, s: (cur(b, n, s), 0)),
    )
    return pl.pallas_call(
        functools.partial(_attn_body, n_kv=n_kv, group=group),
        grid_spec=grid_spec,
        out_shape=jax.ShapeDtypeStruct((t, dq), BF16),
        compiler_params=_cparams(("parallel", "arbitrary")),
        name="swa_attention",
    )(sinks.astype(F32) * LOG2E, qkv, qkv, qkv, qkv, qkv)


def _norm_body(x_ref, g_ref, o_ref):
    x = x_ref[...].astype(F32)
    r = lax.rsqrt(jnp.mean(x * x, axis=-1, keepdims=True) + RMS_EPS)
    o_ref[...] = ((x * r) * g_ref[...]).astype(o_ref.dtype)


def rms_norm(x, g, name):
    t, d = x.shape
    tm = min(1024, t)
    return pl.pallas_call(
        _norm_body,
        grid=(t // tm,),
        in_specs=[pl.BlockSpec((tm, d), lambda i: (i, 0)), pl.BlockSpec((1, d), lambda i: (0, 0))],
        out_specs=pl.BlockSpec((tm, d), lambda i: (i, 0)),
        out_shape=jax.ShapeDtypeStruct((t, d), F32),
        compiler_params=_cparams(("parallel",)),
        name=name,
    )(x, g.reshape(1, d).astype(F32))


def _s5_body(u_ref, win_ref, tp_ref, wout_ref, pwr_ref, pwi_ref, d_ref, o_ref, u2_ref, *, levels):
    L = S5_CHUNK
    rows = u_ref.shape[0] // L
    for s in range(L):
        u2_ref[:, s * LANES:(s + 1) * LANES] = u_ref[pl.ds(s, rows, stride=L), :].astype(BF16)
    u2 = u2_ref[...]
    xin = jnp.dot(u2, win_ref[0], preferred_element_type=F32)
    hw = xin.shape[1] // 2
    hr, hi = xin[:, :hw], xin[:, hw:]
    row = lax.broadcasted_iota(jnp.int32, (rows, hw), 0)
    for lv in range(levels):
        d = 1 << lv
        ar = pwr_ref[0, lv:lv + 1, :]
        ai = pwi_ref[0, lv:lv + 1, :]
        keep = row >= d
        sr = jnp.where(keep, pltpu.roll(hr, d, 0), 0.0)
        si = jnp.where(keep, pltpu.roll(hi, d, 0), 0.0)
        hr, hi = hr + (ar * sr - ai * si), hi + (ar * si + ai * sr)
    keep = row >= 1
    pr = jnp.where(keep, pltpu.roll(hr, 1, 0), 0.0)
    pi_ = jnp.where(keep, pltpu.roll(hi, 1, 0), 0.0)
    hprev = jnp.concatenate([pr, pi_], axis=1).astype(BF16)
    y = (jnp.dot(u2, tp_ref[0], preferred_element_type=F32)
         + jnp.dot(hprev, wout_ref[0], preferred_element_type=F32))
    dsk = d_ref[...]
    for t in range(L):
        yt = y[:, t * LANES:(t + 1) * LANES] + dsk * u_ref[pl.ds(t, rows, stride=L), :]
        o_ref[pl.ds(t, rows, stride=L), :] = 0.5 * yt * (1.0 + lax.erf(yt * (2.0 ** -0.5)))


def _s5_weights(a_re, a_im, log_dt, b_re, b_im, c_re, c_im, levels):
    hp = lax.Precision.HIGHEST
    L = S5_CHUNK
    g, n = a_re.shape
    nc = g * S5_GROUP // LANES
    ga = LANES // S5_GROUP
    dt = jnp.exp(log_dt.astype(F32))[:, None]
    ar = a_re.astype(F32)
    ai = a_im.astype(F32)
    mag = jnp.exp(ar * dt)
    lr = mag * jnp.cos(ai * dt)
    li = mag * jnp.sin(ai * dt)
    den = ar * ar + ai * ai
    zr = ((lr - 1.0) * ar + li * ai) / den
    zi = (li * ar - (lr - 1.0) * ai) / den
    br = b_re.astype(F32)
    bi = b_im.astype(F32)
    bbr = zr[..., None] * br - zi[..., None] * bi
    bbi = zr[..., None] * bi + zi[..., None] * br
    cr = c_re.astype(F32)
    ci = c_im.astype(F32)

    def lam_pow(e):
        e = e.astype(F32)[:, None, None]
        m_ = jnp.exp(e * (ar * dt))
        return m_ * jnp.cos(e * (ai * dt)), m_ * jnp.sin(e * (ai * dt))

    pr, pi_ = lam_pow(jnp.arange(L + 1))
    gr = cr[None] * pr[:L, :, None, :] - ci[None] * pi_[:L, :, None, :]
    gi = cr[None] * pi_[:L, :, None, :] + ci[None] * pr[:L, :, None, :]
    kmat = (jnp.einsum("tgon,gni->tgoi", gr, bbr, precision=hp)
            - jnp.einsum("tgon,gni->tgoi", gi, bbi, precision=hp))
    eye = jnp.eye(ga, dtype=F32)
    kx = jnp.einsum("tcaoi,ab->ctaibo", kmat.reshape(L, nc, ga, S5_GROUP, S5_GROUP), eye)
    kx = kx.reshape(nc, L, LANES, LANES).astype(BF16)
    zero = jnp.zeros((nc, LANES, LANES), BF16)
    tp_c = jnp.concatenate(
        [jnp.concatenate([kx[:, t_ - s_] if t_ >= s_ else zero for t_ in range(L)], axis=2)
         for s_ in range(L)], axis=1)
    qr, qi = pr[L - 1 - jnp.arange(L)], pi_[L - 1 - jnp.arange(L)]
    wr = qr[..., None] * bbr[None] - qi[..., None] * bbi[None]
    wi = qr[..., None] * bbi[None] + qi[..., None] * bbr[None]
    win = jnp.stack([wr, wi]).reshape(2, L, nc, ga, n, S5_GROUP)
    win_c = jnp.einsum("rscani,ab->csairbn", win, eye).reshape(nc, L * LANES, 2 * ga * n)
    er, ei = pr[1:], pi_[1:]
    g_r = cr[None] * er[:, :, None, :] - ci[None] * ei[:, :, None, :]
    g_i = cr[None] * ei[:, :, None, :] + ci[None] * er[:, :, None, :]
    wo = jnp.stack([g_r, -g_i]).reshape(2, L, nc, ga, S5_GROUP, n)
    wout_c = jnp.einsum("rtcaon,ab->crantbo", wo, eye).reshape(nc, 2 * ga * n, L * LANES)
    sr, si = lam_pow(L * (2 ** jnp.arange(levels)))
    pwr = sr.reshape(levels, nc, ga * n).transpose(1, 0, 2)
    pwi = si.reshape(levels, nc, ga * n).transpose(1, 0, 2)
    return win_c.astype(BF16), tp_c, wout_c.astype(BF16), pwr, pwi


def s5_mix(u, weights, d_skip, bsz, seq):
    t, d = u.shape
    win_c, tp_c, wout_c, pwr, pwi = weights
    nc = d // LANES
    levels = pwr.shape[1]
    L = S5_CHUNK
    rows = seq // L
    wspec = lambda a: pl.BlockSpec((1,) + a.shape[1:], lambda c, b: (c, 0, 0))
    return pl.pallas_call(
        functools.partial(_s5_body, levels=levels),
        grid=(nc, bsz),
        in_specs=[
            pl.BlockSpec((seq, LANES), lambda c, b: (b, c)),
            wspec(win_c), wspec(tp_c), wspec(wout_c), wspec(pwr), wspec(pwi),
            pl.BlockSpec((1, LANES), lambda c, b: (0, c)),
        ],
        out_specs=pl.BlockSpec((seq, LANES), lambda c, b: (b, c)),
        out_shape=jax.ShapeDtypeStruct((t, d), F32),
        scratch_shapes=[pltpu.VMEM((rows, L * LANES), BF16)],
        compiler_params=_cparams(("parallel", "arbitrary")),
        name="s5_chunked",
    )(u, win_c, tp_c, wout_c, pwr, pwi, d_skip.reshape(1, d).astype(F32))


def _router_epilogue(accs, biases, acc2, extras):
    logits = accs[0] + biases[0]
    lane = lax.broadcasted_iota(jnp.int32, logits.shape, 1)
    big = jnp.int32(logits.shape[1])
    m1 = jnp.max(logits, axis=-1, keepdims=True)
    i1 = jnp.min(jnp.where(logits == m1, lane, big), axis=-1, keepdims=True)
    rest = jnp.where(lane == i1, -jnp.inf, logits)
    m2 = jnp.max(rest, axis=-1, keepdims=True)
    i2 = jnp.min(jnp.where(rest == m2, lane, big), axis=-1, keepdims=True)
    e2 = jnp.exp(m2 - m1)
    den = 1.0 + e2
    rec = jnp.where(lane == 0, 1.0 / den, 0.0) + jnp.where(lane == 1, e2 / den, 0.0)
    rec = rec + jnp.where(lane == 2, i1.astype(F32), 0.0) + jnp.where(lane == 3, i2.astype(F32), 0.0)
    return rec


def _route_plan(route, n_exp, tm):
    t = route.shape[0]
    e_flat = route[:, 2:4].astype(jnp.int32).T.reshape(-1)
    onehot = (e_flat[:, None] == jnp.arange(n_exp, dtype=jnp.int32)[None, :]).astype(jnp.int32)
    csum = jnp.cumsum(onehot, axis=0)
    rank = jnp.sum(csum * onehot, axis=1) - 1
    counts = csum[-1]
    padded = ((counts + tm - 1) // tm) * tm
    ends = jnp.cumsum(padded)
    starts = ends - padded
    pos = starts[e_flat] + rank
    n_tiles = (2 * t + n_exp * (tm - 1)) // tm
    n_live = (ends[-1] // tm).astype(jnp.int32)
    tile_start = jnp.arange(n_tiles, dtype=jnp.int32) * tm
    tile_expert = jnp.sum((ends[None, :] <= tile_start[:, None]).astype(jnp.int32), axis=1)
    tile_expert = jnp.minimum(tile_expert, tile_expert[n_live - 1]).astype(jnp.int32)
    rows = n_tiles * tm
    tok = jnp.tile(jnp.arange(t, dtype=jnp.int32), 2)
    row_token = jnp.zeros((rows,), jnp.int32).at[pos].set(tok, unique_indices=True)
    return row_token, pos[:t], pos[t:], tile_expert, n_live.reshape(1)


def _gather_body(idx_ref, src_ref, o_ref, sem):
    i = pl.program_id(0)
    n = idx_ref.shape[2]

    def copy(r):
        return pltpu.make_async_copy(src_ref.at[pl.ds(idx_ref[0, 0, r], 1)], o_ref.at[pl.ds(i * n + r, 1)], sem)

    def start(r, c):
        copy(r).start()
        return c

    def wait(r, c):
        copy(r).wait()
        return c

    lax.fori_loop(0, n, start, 0, unroll=8)

    @pl.when(i > 0)
    def _():
        lax.fori_loop(0, n, wait, 0, unroll=8)

    @pl.when(i == pl.num_programs(0) - 1)
    def _():
        lax.fori_loop(0, n, wait, 0, unroll=8)


def gather_rows(src, row_idx):
    rows = row_idx.shape[0]
    d = src.shape[1]
    tg = _fit(GATHER_ROWS, rows)
    return pl.pallas_call(
        _gather_body,
        grid=(rows // tg,),
        in_specs=[
            pl.BlockSpec((1, 1, tg), lambda i: (i, 0, 0), memory_space=pltpu.SMEM),
            pl.BlockSpec(memory_space=pl.ANY),
        ],
        out_specs=pl.BlockSpec(memory_space=pl.ANY),
        out_shape=jax.ShapeDtypeStruct((rows, d), src.dtype),
        scratch_shapes=[pltpu.SemaphoreType.DMA(())],
        compiler_params=_cparams(("arbitrary",)),
        name="moe_gather",
    )(row_idx.reshape(rows // tg, 1, tg), src)


def _combine_body(i1_ref, i2_ref, ys_ref, route_ref, h_ref, o_ref, buf_ref, sem):
    n = o_ref.shape[0]

    def copies(r):
        return (pltpu.make_async_copy(ys_ref.at[pl.ds(i1_ref[0, 0, r], 1)], buf_ref.at[0, pl.ds(r, 1)], sem.at[0]),
                pltpu.make_async_copy(ys_ref.at[pl.ds(i2_ref[0, 0, r], 1)], buf_ref.at[1, pl.ds(r, 1)], sem.at[1]))

    def start(r, c):
        for cp in copies(r):
            cp.start()
        return c

    def wait(r, c):
        for cp in copies(r):
            cp.wait()
        return c

    lax.fori_loop(0, n, start, 0, unroll=8)
    lax.fori_loop(0, n, wait, 0, unroll=8)
    gates = route_ref[...]
    o_ref[...] = h_ref[...] + (gates[:, 0:1] * buf_ref[0] + gates[:, 1:2] * buf_ref[1])


def combine_rows(h, ys, route, pos1, pos2):
    t, d = h.shape
    tg = _fit(GATHER_ROWS, t)
    ispec = pl.BlockSpec((1, 1, tg), lambda i: (i, 0, 0), memory_space=pltpu.SMEM)
    return pl.pallas_call(
        _combine_body,
        grid=(t // tg,),
        in_specs=[ispec, ispec, pl.BlockSpec(memory_space=pl.ANY),
                  pl.BlockSpec((tg, route.shape[1]), lambda i: (i, 0)),
                  pl.BlockSpec((tg, d), lambda i: (i, 0))],
        out_specs=pl.BlockSpec((tg, d), lambda i: (i, 0)),
        out_shape=jax.ShapeDtypeStruct((t, d), h.dtype),
        scratch_shapes=[pltpu.VMEM((2, tg, d), ys.dtype), pltpu.SemaphoreType.DMA((2,))],
        compiler_params=_cparams(("arbitrary",)),
        name="moe_combine",
    )(pos1.reshape(t // tg, 1, tg), pos2.reshape(t // tg, 1, tg), ys, route, h)


def kernel(x, p, positions, norm_mix, norm_ffn, norm_ple, norm_final, attn_w_qkv, attn_b_qkv, attn_w_o, attn_b_o, attn_sinks, s5_a_re, s5_a_im, s5_log_dt, s5_b_re, s5_b_im, s5_c_re, s5_c_im, s5_d, s5_w_glu, s5_b_glu, ffn_w_gate_up, ffn_w_down, moe_w_router, moe_b_router, moe_w_gate_up, moe_w_down, ple_w_proj, ple_w_gate):
    bsz, seq, d = x.shape
    depth = p.shape[0]
    t = bsz * seq
    d_ff = ffn_w_down.shape[1]
    n_exp = moe_w_router.shape[-1]
    qkv_w = attn_w_qkv.shape[-1]
    dq = attn_w_o.shape[1]
    tm = 1024

    h = x.reshape(t, d).astype(F32)
    rope_c, rope_s1, rope_s2 = rope_tables(positions)
    col = jnp.arange(qkv_w)
    rope_cols = (col < dq + (qkv_w - dq) // 2).astype(F32).reshape(1, qkv_w)
    q_scale = jnp.where(col < dq, (HEAD_DIM ** -0.5) * LOG2E, 1.0).astype(F32).reshape(1, qkv_w)
    levels = max(1, int(math.ceil(math.log2(seq // S5_CHUNK))))
    w_qkv, w_o = attn_w_qkv.astype(BF16), attn_w_o.astype(BF16)
    w_ffn_up, w_ffn_down = ffn_w_gate_up.astype(BF16), ffn_w_down.astype(BF16)
    w_glu = s5_w_glu.astype(BF16)
    w_moe_up, w_moe_down = moe_w_gate_up.astype(BF16), moe_w_down.astype(BF16)
    w_ple_gate, w_ple_proj = ple_w_gate.astype(BF16), ple_w_proj.astype(BF16)
    p_rows = p.reshape(depth, t, -1).astype(BF16)

    def resid_add(accs, biases, acc2, extras):
        return extras[0] + accs[0]

    def swiglu_up(accs, biases, acc2, extras):
        return jax.nn.silu(accs[0]) * accs[1]

    def ple_ep(accs, biases, acc2, extras):
        return extras[0] + jax.nn.sigmoid(accs[0]) * acc2

    for i in range(depth):
        k = i // 2
        if i % 2 == 0:
            qkv = fused_mm(
                h, [(w_qkv, 0, (k,))], tm=tm, tn=512, n_out=qkv_w,
                gain=norm_mix[i], biases=[(attn_b_qkv[k], 0)],
                extras=[(rope_c, "rows"), (rope_s1, "rows"), (rope_s2, "rows"), (rope_cols, "cols"),
                        (q_scale, "cols")],
                epilogue=_qkv_epilogue, out_dtype=BF16, name="qkv_rope")
            o = attention(qkv, attn_sinks[k], bsz, seq)
            h = fused_mm(
                o, [(w_o, 0, (k,))], tm=tm, tn=1024, n_out=d,
                biases=[(attn_b_o[k], 0)], extras=[(h, "tile")],
                epilogue=lambda accs, biases, acc2, extras: extras[0] + (accs[0] + biases[0]),
                out_dtype=F32, name="attn_out")
            a = fused_mm(h, [(w_ffn_up, 0, (k,)), (w_ffn_up, d_ff, (k,))], tm=tm, tn=512, n_out=d_ff,
                         gain=norm_ffn[i], epilogue=swiglu_up, out_dtype=BF16, name="ffn_up")
            h = fused_mm(a, [(w_ffn_down, 0, (k,))], tm=tm, tn=256, n_out=d,
                         extras=[(h, "tile")], epilogue=resid_add, out_dtype=F32,
                         name="ffn_down")
        else:
            u = rms_norm(h, norm_mix[i], "s5_norm")
            weights = _s5_weights(s5_a_re[k], s5_a_im[k], s5_log_dt[k], s5_b_re[k], s5_b_im[k],
                                  s5_c_re[k], s5_c_im[k], levels)
            y = s5_mix(u, weights, s5_d[k], bsz, seq)
            h = fused_mm(
                y, [(w_glu, 0, (k,)), (w_glu, d, (k,))], tm=tm, tn=512, n_out=d,
                biases=[(s5_b_glu[k], 0), (s5_b_glu[k], d)], extras=[(h, "tile")],
                epilogue=lambda accs, biases, acc2, extras: extras[0] + (accs[0] + biases[0]) * jax.nn.sigmoid(accs[1] + biases[1]),
                out_dtype=F32, name="s5_glu")
            w_r = jnp.zeros((d, LANES), F32).at[:, :n_exp].set(moe_w_router[k].astype(F32))
            b_r = jnp.full((LANES,), -jnp.inf, F32).at[:n_exp].set(moe_b_router[k].astype(F32))
            route = fused_mm(h, [(w_r.astype(BF16), 0, ())], tm=tm, tn=LANES, n_out=LANES, gain=norm_ffn[i],
                             biases=[(b_r, 0)], epilogue=_router_epilogue, out_dtype=F32, name="router")
            row_token, pos1, pos2, tile_expert, n_live = _route_plan(route, n_exp, MOE_TM)
            xs = gather_rows(h, row_token)
            a = fused_mm(xs, [(w_moe_up, 0, (k,)), (w_moe_up, d_ff, (k,))], tm=MOE_TM, tn=1024, n_out=d_ff,
                         gain=norm_ffn[i], group=(tile_expert, n_live),
                         epilogue=swiglu_up, out_dtype=BF16, name="moe_up")
            ys = fused_mm(a, [(w_moe_down, 0, (k,))], tm=MOE_TM, tn=512, n_out=d,
                          group=(tile_expert, n_live),
                          epilogue=lambda accs, biases, acc2, extras: accs[0], out_dtype=F32,
                          name="moe_down")
            h = combine_rows(h, ys, route, pos1, pos2)
        h = fused_mm(
            h, [(w_ple_gate, 0, (i,))], tm=tm, tn=512, n_out=d, gain=norm_ple[i],
            pair2=(p_rows[i], w_ple_proj[i]), x_cols=True, epilogue=ple_ep, out_dtype=F32, name="ple")
    out = rms_norm(h, norm_final, "final_norm")
    return out.reshape(bsz, seq, d).astype(x.dtype)
```

```python
import functools
import math

import jax
import jax.numpy as jnp
from jax import lax
from jax.experimental import pallas as pl
from jax.experimental.pallas import tpu as pltpu

F32 = jnp.float32
BF16 = jnp.bfloat16

HEAD_DIM = 64
KV_RATIO = 8
WINDOW = 128
ATTN_BLOCK = 128
ROT_DIM = HEAD_DIM // 4
ROPE_THETA = 500000.0
S5_GROUP = 16
S5_CHUNK = 8
RMS_EPS = 1e-6
LOG2E = math.log2(math.e)
LANES = 128
MOE_TM = 512
GATHER_ROWS = 256
VMEM_LIMIT = 60 * 1024 * 1024


def _cparams(sem):
    return pltpu.CompilerParams(dimension_semantics=sem, vmem_limit_bytes=VMEM_LIMIT)


def _fit(tile, n):
    if n <= tile:
        return n
    return max(c for c in range(LANES, tile + 1, LANES) if n % c == 0)


def _mm_body(*refs, norm, stage, grouped, n_rhs, n_bias, pair2, n_extra, nk, epilogue):
    it = iter(refs)
    nt_ref = None
    if grouped:
        next(it)
        nt_ref = next(it)
    x_ref = next(it)
    g_ref = next(it) if norm else None
    w_refs = [next(it) for _ in range(n_rhs)]
    b_refs = [next(it) for _ in range(n_bias)]
    x2_ref, w2_ref = (next(it), next(it)) if pair2 else (None, None)
    e_refs = [next(it) for _ in range(n_extra)]
    o_ref = next(it)
    xs_ref = next(it) if stage else None
    acc_refs = [next(it) for _ in range(n_rhs)] if nk > 1 else []

    i = pl.program_id(0)
    j = pl.program_id(1)
    k = pl.program_id(2)

    def compute():
        if stage:
            @pl.when(j == 0)
            def _():
                x = x_ref[...].astype(F32)
                if norm:
                    r = lax.rsqrt(jnp.mean(x * x, axis=-1, keepdims=True) + RMS_EPS)
                    x = (x * r) * g_ref[...]
                xs_ref[...] = x.astype(BF16)
            lhs = xs_ref[...]
        else:
            lhs = x_ref[...].astype(BF16)

        parts = [jnp.dot(lhs, w[...], preferred_element_type=F32) for w in w_refs]

        def finish(accs):
            biases = [b[...] for b in b_refs]
            acc2 = None
            if pair2:
                acc2 = jnp.dot(x2_ref[...].astype(BF16), w2_ref[...], preferred_element_type=F32)
            o_ref[...] = epilogue(accs, biases, acc2, [e[...] for e in e_refs]).astype(o_ref.dtype)

        if nk == 1:
            finish(parts)
        else:
            @pl.when(k == 0)
            def _():
                for a, p_ in zip(acc_refs, parts):
                    a[...] = p_

            @pl.when(k > 0)
            def _():
                for a, p_ in zip(acc_refs, parts):
                    a[...] += p_

            @pl.when(k == nk - 1)
            def _():
                finish([a[...] for a in acc_refs])

    if grouped:
        live = i < nt_ref[0]
        pl.when(live)(compute)

        @pl.when(jnp.logical_not(live) & (k == nk - 1))
        def _():
            o_ref[...] = jnp.zeros(o_ref.shape, o_ref.dtype)
    else:
        compute()


def fused_mm(x, ws, *, tm, tn, n_out, epilogue, out_dtype, gain=None, biases=(),
             pair2=None, extras=(), tk=None, group=None, name=None):
    m, kdim = x.shape
    tm = _fit(tm, m)
    tn = _fit(tn, n_out)
    tk = kdim if tk is None else _fit(tk, kdim)
    norm = gain is not None
    assert m % tm == 0 and n_out % tn == 0 and kdim % tk == 0
    nk = kdim // tk
    nj = n_out // tn
    assert not (norm and nk > 1)
    stage = norm or (nk == 1 and x.dtype != BF16)
    grouped = group is not None
    grid = (m // tm, nj, nk)

    if grouped:
        def ijk(i, j, k, pf):
            live = i < pf[1][0]
            return (jnp.minimum(i, pf[1][0] - 1), jnp.where(live, j, nj - 1), jnp.where(live, k, nk - 1))
    else:
        def ijk(i, j, k, pf):
            return i, j, k

    def x_map(i, j, k, *pf):
        i, j, k = ijk(i, j, k, pf)
        return (i, k)

    def w_map(ob, lead):
        def f(i, j, k, *pf):
            _, jj, kk = ijk(i, j, k, pf)
            expert = (pf[0][i],) if grouped else ()
            return lead + expert + (kk, jj + ob)
        return f

    def col_map(ob):
        def f(i, j, k, *pf):
            i, j, k = ijk(i, j, k, pf)
            return (0, j + ob)
        return f

    def row_map(i, j, k, *pf):
        i, j, k = ijk(i, j, k, pf)
        return (i, 0)

    def tile_map(i, j, k, *pf):
        i, j, k = ijk(i, j, k, pf)
        return (i, j)

    in_specs = [pl.BlockSpec((tm, tk), x_map)]
    args = [x]
    if norm:
        in_specs.append(pl.BlockSpec((1, kdim), lambda i, j, k, *pf: (0, 0)))
        args.append(gain.reshape(1, kdim).astype(F32))
    for w, off, lead in ws:
        assert off % tn == 0 and w.ndim == len(lead) + (3 if grouped else 2)
        wshape = (None,) * (w.ndim - 2) + (tk, tn)
        in_specs.append(pl.BlockSpec(wshape, w_map(off // tn, tuple(lead))))
        args.append(w)
    for b, off in biases:
        in_specs.append(pl.BlockSpec((1, tn), col_map(off // tn)))
        args.append(b.reshape(1, -1).astype(F32))
    if pair2 is not None:
        x2, w2 = pair2
        k2 = x2.shape[1]
        in_specs.append(pl.BlockSpec((tm, k2), row_map))
        in_specs.append(pl.BlockSpec((k2, tn), col_map(0)))
        args += [x2, w2]
    for arr, kind in extras:
        if kind == "tile":
            in_specs.append(pl.BlockSpec((tm, tn), tile_map))
        elif kind == "rows":
            in_specs.append(pl.BlockSpec((tm, arr.shape[1]), row_map))
        else:
            in_specs.append(pl.BlockSpec((1, tn), col_map(0)))
        args.append(arr)

    scratch = []
    if stage:
        scratch.append(pltpu.VMEM((tm, kdim), BF16))
    if nk > 1:
        scratch += [pltpu.VMEM((tm, tn), F32) for _ in ws]

    body = functools.partial(_mm_body, norm=norm, stage=stage, grouped=grouped, n_rhs=len(ws),
                             n_bias=len(biases), pair2=pair2 is not None, n_extra=len(extras),
                             nk=nk, epilogue=epilogue)
    grid_spec = pltpu.PrefetchScalarGridSpec(
        num_scalar_prefetch=2 if grouped else 0,
        grid=grid,
        in_specs=in_specs,
        out_specs=pl.BlockSpec((tm, tn), lambda i, j, k, *pf: (i, j)),
        scratch_shapes=scratch,
    )
    return pl.pallas_call(
        body,
        grid_spec=grid_spec,
        out_shape=jax.ShapeDtypeStruct((m, n_out), out_dtype),
        compiler_params=_cparams(("parallel", "arbitrary", "arbitrary")),
        name=name,
    )(*(tuple(group) if grouped else ()), *args)


def _rope_table_body(pos_ref, freq_ref, c_ref, s1_ref, s2_ref):
    ang = pos_ref[...].astype(F32) * freq_ref[...]
    lane = lax.broadcasted_iota(jnp.int32, ang.shape, 1) % HEAD_DIM
    cos = jnp.cos(ang)
    sin = jnp.sin(ang)
    half = ROT_DIM // 2
    c_ref[...] = jnp.where(lane < ROT_DIM, cos, 1.0)
    s1_ref[...] = jnp.where(lane < half, -sin, 0.0)
    s2_ref[...] = jnp.where((lane >= half) & (lane < ROT_DIM), sin, 0.0)


def rope_tables(positions):
    t = positions.size
    tm = min(1024, t)
    inv_freq = jnp.power(ROPE_THETA, -jnp.arange(0, ROT_DIM, 2, dtype=F32) / ROT_DIM)
    lane = jnp.arange(LANES) % HEAD_DIM
    freq = jnp.where(lane < ROT_DIM, inv_freq[lane % (ROT_DIM // 2)], 0.0).reshape(1, LANES)
    out = jax.ShapeDtypeStruct((t, LANES), F32)
    spec = pl.BlockSpec((tm, LANES), lambda i: (i, 0))
    return pl.pallas_call(
        _rope_table_body,
        grid=(t // tm,),
        in_specs=[pl.BlockSpec((tm, 1), lambda i: (i, 0)), pl.BlockSpec((1, LANES), lambda i: (0, 0))],
        out_specs=[spec, spec, spec],
        out_shape=[out, out, out],
        compiler_params=_cparams(("parallel",)),
        name="rope_tables",
    )(positions.reshape(t, 1), freq)


def _qkv_epilogue(accs, biases, acc2, extras):
    y = accs[0] + biases[0]
    c, s1, s2, m, sc = extras
    half = ROT_DIM // 2
    outs = []
    for ci in range(y.shape[1] // LANES):
        cols = slice(ci * LANES, (ci + 1) * LANES)
        xc = y[:, cols]
        rot = xc * c + pltpu.roll(xc, LANES - half, 1) * s1 + pltpu.roll(xc, half, 1) * s2
        outs.append((xc + m[:, cols] * (rot - xc)) * sc[:, cols])
    return jnp.concatenate(outs, axis=1)


def _attn_body(sink_ref, q_ref, kp_ref, kc_ref, vp_ref, vc_ref, o_ref, *, n_kv, group):
    n = pl.program_id(1)
    blk = ATTN_BLOCK
    cols = group * blk
    ki = lax.broadcasted_iota(jnp.int32, (2 * blk, cols), 0)
    qi = lax.broadcasted_iota(jnp.int32, (2 * blk, cols), 1) % blk
    rel = qi + blk - ki
    mask = (rel >= 0) & (rel < WINDOW) & ((n > 0) | (ki >= blk))
    vpt = vp_ref[...].astype(F32).T.astype(BF16)
    vct = vc_ref[...].astype(F32).T.astype(BF16)
    for h in range(n_kv):
        cs = slice(h * HEAD_DIM, (h + 1) * HEAD_DIM)
        kb = jnp.concatenate([kp_ref[:, cs], kc_ref[:, cs]], axis=0)
        vt = jnp.concatenate([vpt[cs, :], vct[cs, :]], axis=1)
        q = jnp.concatenate(
            [q_ref[:, (h * group + g) * HEAD_DIM:(h * group + g + 1) * HEAD_DIM] for g in range(group)],
            axis=0)
        sink = jnp.concatenate(
            [jnp.full((1, blk), sink_ref[h * group + g], F32) for g in range(group)], axis=1)
        st = lax.dot_general(kb, q, (((1,), (1,)), ((), ())), preferred_element_type=F32)
        st = jnp.where(mask, st, -jnp.inf)
        mx = jnp.maximum(jnp.max(st, axis=0, keepdims=True), sink)
        e = jnp.exp2(st - mx)
        den = jnp.sum(e, axis=0, keepdims=True) + jnp.exp2(sink - mx)
        ot = jnp.dot(vt, e.astype(BF16), preferred_element_type=F32) / den
        for g2 in range(group // 2):
            pair = jnp.concatenate([ot[:, (2 * g2) * blk:(2 * g2 + 1) * blk],
                                    ot[:, (2 * g2 + 1) * blk:(2 * g2 + 2) * blk]], axis=0)
            c0 = (h * group + 2 * g2) * HEAD_DIM
            o_ref[:, c0:c0 + 2 * HEAD_DIM] = pair.T.astype(o_ref.dtype)


def attention(qkv, sinks, bsz, seq):
    t, width = qkv.shape
    n_q = sinks.shape[0]
    n_kv = n_q // KV_RATIO if n_q >= KV_RATIO else 1
    group = n_q // n_kv
    dq = n_q * HEAD_DIM
    dkv = n_kv * HEAD_DIM
    assert width == dq + 2 * dkv and dq % dkv == 0 and group % 2 == 0
    nb = seq // ATTN_BLOCK
    kblk = dq // dkv
    cur = lambda b, n, s: (b * nb + n)
    prev = lambda b, n, s: (b * nb + jnp.maximum(n - 1, 0))
    grid_spec = pltpu.PrefetchScalarGridSpec(
        num_scalar_prefetch=1,
        grid=(bsz, nb),
        in_specs=[
            pl.BlockSpec((ATTN_BLOCK, dq), lambda b, n, s: (cur(b, n, s), 0)),
            pl.BlockSpec((ATTN_BLOCK, dkv), lambda b, n, s: (prev(b, n, s), kblk)),
            pl.BlockSpec((ATTN_BLOCK, dkv), lambda b, n, s: (cur(b, n, s), kblk)),
            pl.BlockSpec((ATTN_BLOCK, dkv), lambda b, n, s: (prev(b, n, s), kblk + 1)),
            pl.BlockSpec((ATTN_BLOCK, dkv), lambda b, n, s: (cur(b, n, s), kblk + 1)),
        ],
        out_specs=pl.BlockSpec((ATTN_BLOCK, dq), lambda b, n, s: (cur(b, n, s), 0)),
    )
    return pl.pallas_call(
        functools.partial(_attn_body, n_kv=n_kv, group=group),
        grid_spec=grid_spec,
        out_shape=jax.ShapeDtypeStruct((t, dq), BF16),
        compiler_params=_cparams(("parallel", "arbitrary")),
        name="swa_attention",
    )(sinks.astype(F32) * LOG2E, qkv, qkv, qkv, qkv, qkv)


def _norm_body(x_ref, g_ref, o_ref):
    x = x_ref[...].astype(F32)
    r = lax.rsqrt(jnp.mean(x * x, axis=-1, keepdims=True) + RMS_EPS)
    o_ref[...] = ((x * r) * g_ref[...]).astype(o_ref.dtype)


def rms_norm(x, g, name):
    t, d = x.shape
    tm = min(1024, t)
    return pl.pallas_call(
        _norm_body,
        grid=(t // tm,),
        in_specs=[pl.BlockSpec((tm, d), lambda i: (i, 0)), pl.BlockSpec((1, d), lambda i: (0, 0))],
        out_specs=pl.BlockSpec((tm, d), lambda i: (i, 0)),
        out_shape=jax.ShapeDtypeStruct((t, d), F32),
        compiler_params=_cparams(("parallel",)),
        name=name,
    )(x, g.reshape(1, d).astype(F32))


def _s5_body(u_ref, win_ref, tp_ref, wout_ref, pwr_ref, pwi_ref, d_ref, o_ref, u2_ref, *, levels):
    L = S5_CHUNK
    rows = u_ref.shape[0] // L
    for s in range(L):
        u2_ref[:, s * LANES:(s + 1) * LANES] = u_ref[pl.ds(s, rows, stride=L), :].astype(BF16)
    u2 = u2_ref[...]
    xin = jnp.dot(u2, win_ref[0], preferred_element_type=F32)
    hw = xin.shape[1] // 2
    hr, hi = xin[:, :hw], xin[:, hw:]
    row = lax.broadcasted_iota(jnp.int32, (rows, hw), 0)
    for lv in range(levels):
        d = 1 << lv
        ar = pwr_ref[0, lv:lv + 1, :]
        ai = pwi_ref[0, lv:lv + 1, :]
        keep = row >= d
        sr = jnp.where(keep, pltpu.roll(hr, d, 0), 0.0)
        si = jnp.where(keep, pltpu.roll(hi, d, 0), 0.0)
        hr, hi = hr + (ar * sr - ai * si), hi + (ar * si + ai * sr)
    keep = row >= 1
    pr = jnp.where(keep, pltpu.roll(hr, 1, 0), 0.0)
    pi_ = jnp.where(keep, pltpu.roll(hi, 1, 0), 0.0)
    hprev = jnp.concatenate([pr, pi_], axis=1).astype(BF16)
    y = (jnp.dot(u2, tp_ref[0], preferred_element_type=F32)
         + jnp.dot(hprev, wout_ref[0], preferred_element_type=F32))
    dsk = d_ref[...]
    for t in range(L):
        yt = y[:, t * LANES:(t + 1) * LANES] + dsk * u_ref[pl.ds(t, rows, stride=L), :]
        o_ref[pl.ds(t, rows, stride=L), :] = 0.5 * yt * (1.0 + lax.erf(yt * (2.0 ** -0.5)))


def _s5_weights(a_re, a_im, log_dt, b_re, b_im, c_re, c_im, levels):
    hp = lax.Precision.HIGHEST
    L = S5_CHUNK
    g, n = a_re.shape
    nc = g * S5_GROUP // LANES
    ga = LANES // S5_GROUP
    dt = jnp.exp(log_dt.astype(F32))[:, None]
    ar = a_re.astype(F32)
    ai = a_im.astype(F32)
    mag = jnp.exp(ar * dt)
    lr = mag * jnp.cos(ai * dt)
    li = mag * jnp.sin(ai * dt)
    den = ar * ar + ai * ai
    zr = ((lr - 1.0) * ar + li * ai) / den
    zi = (li * ar - (lr - 1.0) * ai) / den
    br = b_re.astype(F32)
    bi = b_im.astype(F32)
    bbr = zr[..., None] * br - zi[..., None] * bi
    bbi = zr[..., None] * bi + zi[..., None] * br
    cr = c_re.astype(F32)
    ci = c_im.astype(F32)

    def lam_pow(e):
        e = e.astype(F32)[:, None, None]
        m_ = jnp.exp(e * (ar * dt))
        return m_ * jnp.cos(e * (ai * dt)), m_ * jnp.sin(e * (ai * dt))

    pr, pi_ = lam_pow(jnp.arange(L + 1))
    gr = cr[None] * pr[:L, :, None, :] - ci[None] * pi_[:L, :, None, :]
    gi = cr[None] * pi_[:L, :, None, :] + ci[None] * pr[:L, :, None, :]
    kmat = (jnp.einsum("tgon,gni->tgoi", gr, bbr, precision=hp)
            - jnp.einsum("tgon,gni->tgoi", gi, bbi, precision=hp))
    eye = jnp.eye(ga, dtype=F32)
    kx = jnp.einsum("tcaoi,ab->ctaibo", kmat.reshape(L, nc, ga, S5_GROUP, S5_GROUP), eye)
    kx = kx.reshape(nc, L, LANES, LANES).astype(BF16)
    zero = jnp.zeros((nc, LANES, LANES), BF16)
    tp_c = jnp.concatenate(
        [jnp.concatenate([kx[:, t_ - s_] if t_ >= s_ else zero for t_ in range(L)], axis=2)
         for s_ in range(L)], axis=1)
    qr, qi = pr[L - 1 - jnp.arange(L)], pi_[L - 1 - jnp.arange(L)]
    wr = qr[..., None] * bbr[None] - qi[..., None] * bbi[None]
    wi = qr[..., None] * bbi[None] + qi[..., None] * bbr[None]
    win = jnp.stack([wr, wi]).reshape(2, L, nc, ga, n, S5_GROUP)
    win_c = jnp.einsum("rscani,ab->csairbn", win, eye).reshape(nc, L * LANES, 2 * ga * n)
    er, ei = pr[1:], pi_[1:]
    g_r = cr[None] * er[:, :, None, :] - ci[None] * ei[:, :, None, :]
    g_i = cr[None] * ei[:, :, None, :] + ci[None] * er[:, :, None, :]
    wo = jnp.stack([g_r, -g_i]).reshape(2, L, nc, ga, S5_GROUP, n)
    wout_c = jnp.einsum("rtcaon,ab->crantbo", wo, eye).reshape(nc, 2 * ga * n, L * LANES)
    sr, si = lam_pow(L * (2 ** jnp.arange(levels)))
    pwr = sr.reshape(levels, nc, ga * n).transpose(1, 0, 2)
    pwi = si.reshape(levels, nc, ga * n).transpose(1, 0, 2)
    return win_c.astype(BF16), tp_c, wout_c.astype(BF16), pwr, pwi


def s5_mix(u, weights, d_skip, bsz, seq):
    t, d = u.shape
    win_c, tp_c, wout_c, pwr, pwi = weights
    nc = d // LANES
    levels = pwr.shape[1]
    L = S5_CHUNK
    rows = seq // L
    wspec = lambda a: pl.BlockSpec((1,) + a.shape[1:], lambda c, b: (c, 0, 0))
    return pl.pallas_call(
        functools.partial(_s5_body, levels=levels),
        grid=(nc, bsz),
        in_specs=[
            pl.BlockSpec((seq, LANES), lambda c, b: (b, c)),
            wspec(win_c), wspec(tp_c), wspec(wout_c), wspec(pwr), wspec(pwi),
            pl.BlockSpec((1, LANES), lambda c, b: (0, c)),
        ],
        out_specs=pl.BlockSpec((seq, LANES), lambda c, b: (b, c)),
        out_shape=jax.ShapeDtypeStruct((t, d), F32),
        scratch_shapes=[pltpu.VMEM((rows, L * LANES), BF16)],
        compiler_params=_cparams(("parallel", "arbitrary")),
        name="s5_chunked",
    )(u, win_c, tp_c, wout_c, pwr, pwi, d_skip.reshape(1, d).astype(F32))


def _router_epilogue(accs, biases, acc2, extras):
    logits = accs[0] + biases[0]
    lane = lax.broadcasted_iota(jnp.int32, logits.shape, 1)
    big = jnp.int32(logits.shape[1])
    m1 = jnp.max(logits, axis=-1, keepdims=True)
    i1 = jnp.min(jnp.where(logits == m1, lane, big), axis=-1, keepdims=True)
    rest = jnp.where(lane == i1, -jnp.inf, logits)
    m2 = jnp.max(rest, axis=-1, keepdims=True)
    i2 = jnp.min(jnp.where(rest == m2, lane, big), axis=-1, keepdims=True)
    e2 = jnp.exp(m2 - m1)
    den = 1.0 + e2
    rec = jnp.where(lane == 0, 1.0 / den, 0.0) + jnp.where(lane == 1, e2 / den, 0.0)
    rec = rec + jnp.where(lane == 2, i1.astype(F32), 0.0) + jnp.where(lane == 3, i2.astype(F32), 0.0)
    return rec


def _route_plan(route, n_exp, tm):
    t = route.shape[0]
    e_flat = route[:, 2:4].astype(jnp.int32).T.reshape(-1)
    onehot = (e_flat[:, None] == jnp.arange(n_exp, dtype=jnp.int32)[None, :]).astype(jnp.int32)
    csum = jnp.cumsum(onehot, axis=0)
    rank = jnp.sum(csum * onehot, axis=1) - 1
    counts = csum[-1]
    padded = ((counts + tm - 1) // tm) * tm
    ends = jnp.cumsum(padded)
    starts = ends - padded
    pos = starts[e_flat] + rank
    n_tiles = (2 * t + n_exp * (tm - 1)) // tm
    n_live = (ends[-1] // tm).astype(jnp.int32)
    tile_start = jnp.arange(n_tiles, dtype=jnp.int32) * tm
    tile_expert = jnp.sum((ends[None, :] <= tile_start[:, None]).astype(jnp.int32), axis=1)
    tile_expert = jnp.minimum(tile_expert, tile_expert[n_live - 1]).astype(jnp.int32)
    rows = n_tiles * tm
    tok = jnp.tile(jnp.arange(t, dtype=jnp.int32), 2)
    row_token = jnp.zeros((rows,), jnp.int32).at[pos].set(tok, unique_indices=True)
    return row_token, pos[:t], pos[t:], tile_expert, n_live.reshape(1)


def _gather_body(idx_ref, src_ref, o_ref, sem):
    n = o_ref.shape[0]

    def copy(r):
        return pltpu.make_async_copy(src_ref.at[pl.ds(idx_ref[0, 0, r], 1)], o_ref.at[pl.ds(r, 1)], sem)

    def start(r2, c):
        copy(2 * r2).start(priority=0)
        copy(2 * r2 + 1).start(priority=1)
        return c

    def wait(r, c):
        copy(r).wait()
        return c

    lax.fori_loop(0, n // 2, start, 0, unroll=4)
    lax.fori_loop(0, n, wait, 0, unroll=8)


def gather_rows(src, row_idx):
    rows = row_idx.shape[0]
    d = src.shape[1]
    tg = _fit(GATHER_ROWS, rows)
    return pl.pallas_call(
        _gather_body,
        grid=(rows // tg,),
        in_specs=[
            pl.BlockSpec((1, 1, tg), lambda i: (i, 0, 0), memory_space=pltpu.SMEM),
            pl.BlockSpec(memory_space=pl.ANY),
        ],
        out_specs=pl.BlockSpec((tg, d), lambda i: (i, 0)),
        out_shape=jax.ShapeDtypeStruct((rows, d), src.dtype),
        scratch_shapes=[pltpu.SemaphoreType.DMA(())],
        compiler_params=_cparams(("arbitrary",)),
        name="moe_gather",
    )(row_idx.reshape(rows // tg, 1, tg), src)


def _combine_body(i1_ref, i2_ref, ys_ref, route_ref, h_ref, o_ref, buf_ref, sem):
    n = o_ref.shape[0]

    def copies(r):
        return (pltpu.make_async_copy(ys_ref.at[pl.ds(i1_ref[0, 0, r], 1)], buf_ref.at[0, pl.ds(r, 1)], sem.at[0]),
                pltpu.make_async_copy(ys_ref.at[pl.ds(i2_ref[0, 0, r], 1)], buf_ref.at[1, pl.ds(r, 1)], sem.at[1]))

    def start(r, c):
        for prio, cp in enumerate(copies(r)):
            cp.start(priority=prio)
        return c

    def wait(r, c):
        for cp in copies(r):
            cp.wait()
        return c

    lax.fori_loop(0, n, start, 0, unroll=8)
    lax.fori_loop(0, n, wait, 0, unroll=8)
    gates = route_ref[...]
    o_ref[...] = h_ref[...] + (gates[:, 0:1] * buf_ref[0] + gates[:, 1:2] * buf_ref[1])


def combine_rows(h, ys, route, pos1, pos2):
    t, d = h.shape
    tg = _fit(GATHER_ROWS, t)
    ispec = pl.BlockSpec((1, 1, tg), lambda i: (i, 0, 0), memory_space=pltpu.SMEM)
    return pl.pallas_call(
        _combine_body,
        grid=(t // tg,),
        in_specs=[ispec, ispec, pl.BlockSpec(memory_space=pl.ANY),
                  pl.BlockSpec((tg, route.shape[1]), lambda i: (i, 0)),
                  pl.BlockSpec((tg, d), lambda i: (i, 0))],
        out_specs=pl.BlockSpec((tg, d), lambda i: (i, 0)),
        out_shape=jax.ShapeDtypeStruct((t, d), h.dtype),
        scratch_shapes=[pltpu.VMEM((2, tg, d), ys.dtype), pltpu.SemaphoreType.DMA((2,))],
        compiler_params=_cparams(("arbitrary",)),
        name="moe_combine",
    )(pos1.reshape(t // tg, 1, tg), pos2.reshape(t // tg, 1, tg), ys, route, h)


def kernel(x, p, positions, norm_mix, norm_ffn, norm_ple, norm_final, attn_w_qkv, attn_b_qkv, attn_w_o, attn_b_o, attn_sinks, s5_a_re, s5_a_im, s5_log_dt, s5_b_re, s5_b_im, s5_c_re, s5_c_im, s5_d, s5_w_glu, s5_b_glu, ffn_w_gate_up, ffn_w_down, moe_w_router, moe_b_router, moe_w_gate_up, moe_w_down, ple_w_proj, ple_w_gate):
    bsz, seq, d = x.shape
    depth = p.shape[0]
    t = bsz * seq
    d_ff = ffn_w_down.shape[1]
    n_exp = moe_w_router.shape[-1]
    qkv_w = attn_w_qkv.shape[-1]
    dq = attn_w_o.shape[1]
    tm = 1024

    h = x.reshape(t, d).astype(F32)
    rope_c, rope_s1, rope_s2 = rope_tables(positions)
    col = jnp.arange(qkv_w)
    rope_cols = (col < dq + (qkv_w - dq) // 2).astype(F32).reshape(1, qkv_w)
    q_scale = jnp.where(col < dq, (HEAD_DIM ** -0.5) * LOG2E, 1.0).astype(F32).reshape(1, qkv_w)
    levels = max(1, int(math.ceil(math.log2(seq // S5_CHUNK))))
    w_qkv, w_o = attn_w_qkv.astype(BF16), attn_w_o.astype(BF16)
    w_ffn_up, w_ffn_down = ffn_w_gate_up.astype(BF16), ffn_w_down.astype(BF16)
    w_glu = s5_w_glu.astype(BF16)
    w_moe_up, w_moe_down = moe_w_gate_up.astype(BF16), moe_w_down.astype(BF16)
    w_ple_gate, w_ple_proj = ple_w_gate.astype(BF16), ple_w_proj.astype(BF16)
    p_rows = p.reshape(depth, t, -1).astype(BF16)

    def resid_add(accs, biases, acc2, extras):
        return extras[0] + accs[0]

    def swiglu_up(accs, biases, acc2, extras):
        return jax.nn.silu(accs[0]) * accs[1]

    def ple_ep(accs, biases, acc2, extras):
        return extras[0] + jax.nn.sigmoid(accs[0]) * acc2

    for i in range(depth):
        k = i // 2
        if i % 2 == 0:
            qkv = fused_mm(
                h, [(w_qkv, 0, (k,))], tm=tm, tn=512, n_out=qkv_w,
                gain=norm_mix[i], biases=[(attn_b_qkv[k], 0)],
                extras=[(rope_c, "rows"), (rope_s1, "rows"), (rope_s2, "rows"), (rope_cols, "cols"),
                        (q_scale, "cols")],
                epilogue=_qkv_epilogue, out_dtype=BF16, name="qkv_rope")
            o = attention(qkv, attn_sinks[k], bsz, seq)
            h = fused_mm(
                o, [(w_o, 0, (k,))], tm=tm, tn=1024, n_out=d,
                biases=[(attn_b_o[k], 0)], extras=[(h, "tile")],
                epilogue=lambda accs, biases, acc2, extras: extras[0] + (accs[0] + biases[0]),
                out_dtype=F32, name="attn_out")
            a = fused_mm(h, [(w_ffn_up, 0, (k,)), (w_ffn_up, d_ff, (k,))], tm=tm, tn=512, n_out=d_ff,
                         gain=norm_ffn[i], epilogue=swiglu_up, out_dtype=BF16, name="ffn_up")
            h = fused_mm(a, [(w_ffn_down, 0, (k,))], tm=512, tn=2048, n_out=d, tk=1792,
                         extras=[(h, "tile")], epilogue=resid_add, out_dtype=F32,
                         name="ffn_down")
        else:
            u = rms_norm(h, norm_mix[i], "s5_norm")
            weights = _s5_weights(s5_a_re[k], s5_a_im[k], s5_log_dt[k], s5_b_re[k], s5_b_im[k],
                                  s5_c_re[k], s5_c_im[k], levels)
            y = s5_mix(u, weights, s5_d[k], bsz, seq)
            h = fused_mm(
                y, [(w_glu, 0, (k,)), (w_glu, d, (k,))], tm=tm, tn=512, n_out=d,
                biases=[(s5_b_glu[k], 0), (s5_b_glu[k], d)], extras=[(h, "tile")],
                epilogue=lambda accs, biases, acc2, extras: extras[0] + (accs[0] + biases[0]) * jax.nn.sigmoid(accs[1] + biases[1]),
                out_dtype=F32, name="s5_glu")
            w_r = jnp.zeros((d, LANES), F32).at[:, :n_exp].set(moe_w_router[k].astype(F32))
            b_r = jnp.full((LANES,), -jnp.inf, F32).at[:n_exp].set(moe_b_router[k].astype(F32))
            route = fused_mm(h, [(w_r.astype(BF16), 0, ())], tm=tm, tn=LANES, n_out=LANES, gain=norm_ffn[i],
                             biases=[(b_r, 0)], epilogue=_router_epilogue, out_dtype=F32, name="router")
            row_token, pos1, pos2, tile_expert, n_live = _route_plan(route, n_exp, MOE_TM)
            xs = gather_rows(h, row_token)
            a = fused_mm(xs, [(w_moe_up, 0, (k,)), (w_moe_up, d_ff, (k,))], tm=MOE_TM, tn=1024, n_out=d_ff,
                         gain=norm_ffn[i], group=(tile_expert, n_live),
                         epilogue=swiglu_up, out_dtype=BF16, name="moe_up")
            ys = fused_mm(a, [(w_moe_down, 0, (k,))], tm=MOE_TM, tn=2048, n_out=d, tk=1792,
                          group=(tile_expert, n_live),
                          epilogue=lambda accs, biases, acc2, extras: accs[0], out_dtype=F32,
                          name="moe_down")
            h = combine_rows(h, ys, route, pos1, pos2)
        h = fused_mm(
            h, [(w_ple_gate, 0, (i,))], tm=tm, tn=512, n_out=d, gain=norm_ple[i],
            pair2=(p_rows[i], w_ple_proj[i]),
            extras=[(h, "tile")], epilogue=ple_ep, out_dtype=F32, name="ple")
    out = rms_norm(h, norm_final, "final_norm")
    return out.reshape(bsz, seq, d).astype(x.dtype)
```
